```python
import math
import jax, jax.numpy as jnp
from jax import lax
import numpy as np

D_MODEL = 1024
BATCH = 4
SEQ = 8192
DEPTH = 4
DEC_BATCH = 16
DEC_SEQ = 16
PAST_LEN = 4096

CHUNK = 64
N_HEADS = 8
HEAD_DIM = 64
V_DIM = 2 * HEAD_DIM
ATTN_W = N_HEADS * V_DIM
CONV_W = 512
CONV_K = 31
POOL_W = 512
POOL_WINDOWS = (2, 4, 8, 16)
POOL_GROUPS = 4
POOL_GW = POOL_W // POOL_GROUPS
POOL_BUF = 15
D_FF = 2816
ROPE_THETA = 10000.0
Q_BLOCK = 128
EPS = 1e-6
N_MOD = 9
IN_W = 2 * CONV_W + POOL_W + 3 * ATTN_W + 3 * D_MODEL
IN_SPLIT = (2 * CONV_W, 2 * CONV_W + POOL_W, 2 * CONV_W + POOL_W + ATTN_W, 2 * CONV_W + POOL_W + 2 * ATTN_W, 2 * CONV_W + POOL_W + 3 * ATTN_W)

kernel_name = 'hybrid_conv_pool_diffattn_stream'


def _rms_norm(x, g):
    xf = x.astype(jnp.float32)
    y = xf * lax.rsqrt(jnp.mean(xf * xf, axis=-1, keepdims=True) + EPS)
    return (y * g.astype(jnp.float32)).astype(x.dtype)


def _layer_norm(x, g, b):
    xf = x.astype(jnp.float32)
    mu = jnp.mean(xf, axis=-1, keepdims=True)
    var = jnp.mean(jnp.square(xf - mu), axis=-1, keepdims=True)
    y = (xf - mu) * lax.rsqrt(var + EPS)
    return (y * g.astype(jnp.float32) + b.astype(jnp.float32)).astype(x.dtype)


def _modulate(h, shift, scale):
    return h * (1.0 + scale[:, None, :]) + shift[:, None, :]


def _rope(x, pos):
    half = HEAD_DIM // 2
    inv = ROPE_THETA ** (-jnp.arange(half, dtype=jnp.float32) / half)
    ang = pos.astype(jnp.float32)[:, None] * inv[None, :]
    cos = jnp.cos(ang)[None, :, None, None, :]
    sin = jnp.sin(ang)[None, :, None, None, :]
    xf = x.astype(jnp.float32)
    x1, x2 = xf[..., :half], xf[..., half:]
    return jnp.concatenate([x1 * cos - x2 * sin, x2 * cos + x1 * sin], axis=-1).astype(x.dtype)


def _swiglu(h, w_in, w_out):
    a, b = jnp.split(h @ w_in, 2, axis=-1)
    return (jax.nn.silu(a) * b) @ w_out


def _conv_branch(u, buf, w_dw, b_dw, ln_g, ln_b, w_proj):
    a, gt = jnp.split(u, 2, axis=-1)
    z = a * jax.nn.sigmoid(gt)
    zc = jnp.concatenate([buf, z], axis=1)
    y = lax.conv_general_dilated(zc, w_dw[:, None, :], (1,), 'VALID',
                                 dimension_numbers=('NWC', 'WIO', 'NWC'),
                                 feature_group_count=CONV_W) + b_dw
    y = jax.nn.silu(_layer_norm(y, ln_g, ln_b))
    return y @ w_proj, zc[:, -(CONV_K - 1):]


def _pool_branch(p, buf, pos, w_grp, scale, w_proj):
    B, T = p.shape[0], p.shape[1]
    pc = jnp.concatenate([buf, p], axis=1)
    cs = jnp.cumsum(pc.astype(jnp.float32), axis=1)
    cs = jnp.concatenate([jnp.zeros_like(cs[:, :1]), cs], axis=1)
    end = cs[:, POOL_BUF + 1:]
    pf = p.astype(jnp.float32)
    outs = []
    for g, w in enumerate(POOL_WINDOWS):
        lo, hi = g * POOL_GW, (g + 1) * POOL_GW
        start = cs[:, POOL_BUF + 1 - w:POOL_BUF + 1 - w + T, lo:hi]
        cnt = jnp.minimum(w, pos + 1).astype(jnp.float32)[None, :, None]
        outs.append((end[..., lo:hi] - start) / cnt - pf[..., lo:hi])
    d = jnp.stack(outs, axis=2).astype(p.dtype)
    m = jnp.einsum('btgi,gio->btgo', d, w_grp).reshape(B, T, POOL_W) * scale
    return m @ w_proj, pc[:, -POOL_BUF:]


def _diff_core(q, k, v, lam, mask):
    s = jnp.einsum('bqhmd,bshmd->bhmqs', q, k).astype(jnp.float32) * (HEAD_DIM ** -0.5)
    if mask is not None:
        s = jnp.where(mask, s, -jnp.inf)
    pr = jax.nn.softmax(s, axis=-1)
    a = pr[:, :, 0] - lam * pr[:, :, 1]
    return jnp.einsum('bhqs,bshv->bqhv', a.astype(v.dtype), v)


def _attn_prompt(q, k, v, lam):
    B, T = q.shape[0], q.shape[1]
    nb = T // Q_BLOCK
    qb = jnp.swapaxes(q.reshape(B, nb, Q_BLOCK, N_HEADS, 2, HEAD_DIM), 0, 1)
    kchunk = jnp.arange(T) // CHUNK

    def block(args):
        qblk, i = args
        qchunk = (i * Q_BLOCK + jnp.arange(Q_BLOCK)) // CHUNK
        mask = kchunk[None, :] <= qchunk[:, None]
        return _diff_core(qblk, k, v, lam, mask)

    o = lax.map(block, (qb, jnp.arange(nb)))
    return jnp.swapaxes(o, 0, 1).reshape(B, T, N_HEADS, V_DIM)


def _layer(x, c, pos, conv_buf, pool_buf, k_past, v_past, p, lam, lam_init):
    B, T = x.shape[0], x.shape[1]
    mod = jax.nn.silu(c) @ p['w_ada'] + p['b_ada']
    sh1, sc1, g1, sh2, sc2, g2, sh3, sc3, g3 = jnp.split(mod, N_MOD, axis=-1)
    h = _modulate(_rms_norm(x, p['g_ffn1']), sh1, sc1)
    x = x + 0.5 * g1[:, None, :] * _swiglu(h, p['w_ffn1_in'], p['w_ffn1_out'])
    h = _modulate(_rms_norm(x, p['g_mix']), sh2, sc2)
    u = h @ p['w_in']
    u_conv, u_pool, u_q, u_k, u_v, u_gate = jnp.split(u, IN_SPLIT, axis=-1)
    y_conv, new_conv = _conv_branch(u_conv, conv_buf, p['w_dw'], p['b_dw'], p['ln_conv_g'], p['ln_conv_b'], p['w_conv_out'])
    y_pool, new_pool = _pool_branch(u_pool, pool_buf, pos, p['w_pool_grp'], p['pool_scale'], p['w_pool_out'])
    q = _rope(_rms_norm(u_q.reshape(B, T, N_HEADS, 2, HEAD_DIM), p['g_q']), pos)
    k = _rope(_rms_norm(u_k.reshape(B, T, N_HEADS, 2, HEAD_DIM), p['g_k']), pos)
    v = u_v.reshape(B, T, N_HEADS, V_DIM)
    if k_past is None:
        o = _attn_prompt(q, k, v, lam)
    else:
        o = _diff_core(q, jnp.concatenate([k_past, k], axis=1), jnp.concatenate([v_past, v], axis=1), lam, None)
    o = _rms_norm(o, p['g_sub']) * (1.0 - lam_init)
    y_attn = o.reshape(B, T, ATTN_W) @ p['w_attn_out']
    gc, gp, ga = jnp.split(jax.nn.sigmoid(u_gate), 3, axis=-1)
    merged = gc * y_conv + gp * y_pool + ga * y_attn
    x = x + g2[:, None, :] * (merged @ p['w_out'])
    h = _modulate(_rms_norm(x, p['g_ffn2']), sh3, sc3)
    x = x + 0.5 * g3[:, None, :] * _swiglu(h, p['w_ffn2_in'], p['w_ffn2_out'])
    return x, k, v, new_conv, new_pool


def setup_inputs(seed: int = 0) -> dict:
    key = jax.random.key(seed)
    ks = iter(jax.random.split(key, 48))
    nrm = lambda shape, s=1.0: jax.random.normal(next(ks), shape, jnp.float32) * s
    gain = lambda shape: 1.0 + 0.02 * jax.random.normal(next(ks), shape, jnp.float32)
    L, D = DEPTH, D_MODEL
    return {
        'x_prompt': nrm((BATCH, SEQ, D)),
        'x_sample': nrm((DEC_BATCH, DEC_SEQ, D)),
        'cache_attn_k': nrm((L, DEC_BATCH, PAST_LEN, N_HEADS, 2, HEAD_DIM)),
        'cache_attn_v': nrm((L, DEC_BATCH, PAST_LEN, N_HEADS, V_DIM)),
        'state_conv': nrm((L, DEC_BATCH, CONV_K - 1, CONV_W), 0.5),
        'state_pool': nrm((L, DEC_BATCH, POOL_BUF, POOL_W)),
        'c_prompt': nrm((BATCH, D)),
        'c_sample': nrm((DEC_BATCH, D)),
        'w_ada': nrm((L, D, N_MOD * D), D ** -0.5),
        'b_ada': nrm((L, N_MOD * D), 0.02),
        'g_ffn1': gain((L, D)),
        'w_ffn1_in': nrm((L, D, 2 * D_FF), D ** -0.5),
        'w_ffn1_out': nrm((L, D_FF, D), D_FF ** -0.5),
        'g_mix': gain((L, D)),
        'w_in': nrm((L, D, IN_W), D ** -0.5),
        'w_dw': nrm((L, CONV_K, CONV_W), CONV_K ** -0.5),
        'b_dw': nrm((L, CONV_W), 0.02),
        'ln_conv_g': gain((L, CONV_W)),
        'ln_conv_b': nrm((L, CONV_W), 0.02),
        'w_conv_out': nrm((L, CONV_W, D), CONV_W ** -0.5),
        'w_pool_grp': nrm((L, POOL_GROUPS, POOL_GW, POOL_GW), POOL_GW ** -0.5),
        'pool_scale': 1.0 + 0.1 * nrm((L, POOL_W)),
        'w_pool_out': nrm((L, POOL_W, D), POOL_W ** -0.5),
        'g_q': gain((L, HEAD_DIM)),
        'g_k': gain((L, HEAD_DIM)),
        'lam_q1': nrm((L, HEAD_DIM), 0.1),
        'lam_k1': nrm((L, HEAD_DIM), 0.1),
        'lam_q2': nrm((L, HEAD_DIM), 0.1),
        'lam_k2': nrm((L, HEAD_DIM), 0.1),
        'g_sub': gain((L, V_DIM)),
        'w_attn_out': nrm((L, ATTN_W, D), ATTN_W ** -0.5),
        'w_out': nrm((L, D, D), D ** -0.5),
        'g_ffn2': gain((L, D)),
        'w_ffn2_in': nrm((L, D, 2 * D_FF), D ** -0.5),
        'w_ffn2_out': nrm((L, D_FF, D), D_FF ** -0.5),
    }


def reference(x_prompt, x_sample, cache_attn_k, cache_attn_v, state_conv, state_pool, c_prompt, c_sample,
              w_ada, b_ada, g_ffn1, w_ffn1_in, w_ffn1_out, g_mix, w_in, w_dw, b_dw, ln_conv_g, ln_conv_b,
              w_conv_out, w_pool_grp, pool_scale, w_pool_out, g_q, g_k, lam_q1, lam_k1, lam_q2, lam_k2,
              g_sub, w_attn_out, w_out, g_ffn2, w_ffn2_in, w_ffn2_out):
    Bp, Tp = x_prompt.shape[0], x_prompt.shape[1]
    past = cache_attn_k.shape[2]
    pos_p = jnp.arange(Tp)
    pos_s = past + jnp.arange(x_sample.shape[1])
    conv0 = jnp.zeros((Bp, CONV_K - 1, CONV_W), x_prompt.dtype)
    pool0 = jnp.zeros((Bp, POOL_BUF, POOL_W), x_prompt.dtype)
    xp, xs = x_prompt, x_sample
    kp_l, vp_l, cp_l, pp_l, ks_l, vs_l, cs_l, ps_l = [], [], [], [], [], [], [], []
    for l in range(DEPTH):
        lam_init = 0.8 - 0.6 * math.exp(-0.3 * l)
        lam = (jnp.exp(jnp.sum(lam_q1[l].astype(jnp.float32) * lam_k1[l].astype(jnp.float32)))
               - jnp.exp(jnp.sum(lam_q2[l].astype(jnp.float32) * lam_k2[l].astype(jnp.float32))) + lam_init)
        p = dict(w_ada=w_ada[l], b_ada=b_ada[l], g_ffn1=g_ffn1[l], w_ffn1_in=w_ffn1_in[l], w_ffn1_out=w_ffn1_out[l],
                 g_mix=g_mix[l], w_in=w_in[l], w_dw=w_dw[l], b_dw=b_dw[l], ln_conv_g=ln_conv_g[l],
                 ln_conv_b=ln_conv_b[l], w_conv_out=w_conv_out[l], w_pool_grp=w_pool_grp[l],
                 pool_scale=pool_scale[l], w_pool_out=w_pool_out[l], g_q=g_q[l], g_k=g_k[l], g_sub=g_sub[l],
                 w_attn_out=w_attn_out[l], w_out=w_out[l], g_ffn2=g_ffn2[l], w_ffn2_in=w_ffn2_in[l],
                 w_ffn2_out=w_ffn2_out[l])
        xp, kp, vp, cp, pp = _layer(xp, c_prompt, pos_p, conv0, pool0, None, None, p, lam, lam_init)
        xs, kn, vn, cn, pn = _layer(xs, c_sample, pos_s, state_conv[l], state_pool[l],
                                    cache_attn_k[l], cache_attn_v[l], p, lam, lam_init)
        kp_l.append(kp); vp_l.append(vp); cp_l.append(cp); pp_l.append(pp)
        ks_l.append(kn); vs_l.append(vn); cs_l.append(cn); ps_l.append(pn)
    return (xp, xs,
            jnp.stack(kp_l), jnp.stack(vp_l), jnp.stack(cp_l), jnp.stack(pp_l),
            jnp.stack(ks_l), jnp.stack(vs_l), jnp.stack(cs_l), jnp.stack(ps_l))
```

```python
import functools
import math

import numpy as np
import jax
import jax.numpy as jnp
from jax import lax
from jax.experimental import pallas as pl
from jax.experimental.pallas import tpu as pltpu

F32 = jnp.float32
BF16 = jnp.bfloat16

EPS = 1e-6
CHUNK = 64
N_HEADS = 8
HEAD_DIM = 64
V_DIM = 2 * HEAD_DIM
CONV_K = 31
POOL_WINDOWS = (2, 4, 8, 16)
POOL_BUF = 15
ROPE_THETA = 10000.0
N_MOD = 9

CONV_HALO = 32
POOL_HALO = 16
LANES = 128
V7X_VMEM_LIMIT = 56 * 1024 * 1024


def _params(semantics):
    return pltpu.CompilerParams(dimension_semantics=semantics, vmem_limit_bytes=V7X_VMEM_LIMIT)


def _dot(a, b):
    return jnp.dot(a, b, preferred_element_type=F32)


def _sigmoid(x):
    return 1.0 / (1.0 + jnp.exp(-x))


def _rms_mod(x, g, sh, sc):
    ms = jnp.mean(x * x, axis=-1, keepdims=True)
    y = x * lax.rsqrt(ms + EPS) * g
    return y * (1.0 + sc) + sh


def _row_spec(arr, tm):
    d = arr.shape[-1]
    if arr.shape[1] == 1:
        return pl.BlockSpec((1, 1, d), lambda b, i: (b, 0, 0))
    return pl.BlockSpec((1, tm, d), lambda b, i: (b, i, 0))


def _resident(shape, index_map):
    return pl.BlockSpec(shape, index_map, pipeline_mode=pl.Buffered(1))


def _ada_kernel(c_ref, w_ref, b_ref, o_ref):
    c = c_ref[...]
    a = (c * _sigmoid(c)).astype(BF16)
    o_ref[0] = _dot(a, w_ref[0].astype(BF16)) + b_ref[0]


def _ada(c_all, w_ada, b_ada):
    n_layers, d, n = w_ada.shape
    rows = c_all.shape[0]
    tn = 1024
    return pl.pallas_call(
        _ada_kernel,
        out_shape=jax.ShapeDtypeStruct((n_layers, rows, n), F32),
        grid=(n_layers, n // tn),
        in_specs=[
            pl.BlockSpec((rows, d), lambda l, j: (0, 0)),
            pl.BlockSpec((1, d, tn), lambda l, j: (l, 0, j)),
            pl.BlockSpec((1, 1, tn), lambda l, j: (l, 0, j)),
        ],
        out_specs=pl.BlockSpec((1, rows, tn), lambda l, j: (l, 0, j)),
        compiler_params=_params(("parallel", "parallel")),
        name="ada",
    )(c_all, w_ada, b_ada.reshape(n_layers, 1, n))


def _ffn_kernel(x_ref, g_ref, sh_ref, sc_ref, gt_ref, win_ref, wout_ref, o_ref, *, d_ff):
    x = x_ref[0]
    h = _rms_mod(x, g_ref[...], sh_ref[0], sc_ref[0]).astype(BF16)
    a = _dot(h, win_ref[0, :, :d_ff])
    b = _dot(h, win_ref[0, :, d_ff:])
    hid = (a * _sigmoid(a) * b).astype(BF16)
    y = _dot(hid, wout_ref[0])
    o_ref[0] = x + 0.5 * gt_ref[0] * y


def _ffn(x, g, sh, sc, gate, w_in, w_out, layer, tm):
    nb, tb, d = x.shape
    d_ff = w_out.shape[1]
    return pl.pallas_call(
        functools.partial(_ffn_kernel, d_ff=d_ff),
        out_shape=jax.ShapeDtypeStruct(x.shape, F32),
        grid=(nb, tb // tm),
        in_specs=[
            pl.BlockSpec((1, tm, d), lambda b, i: (b, i, 0)),
            pl.BlockSpec((1, d), lambda b, i: (0, 0)),
            _row_spec(sh, tm), _row_spec(sc, tm), _row_spec(gate, tm),
            _resident((1, d, 2 * d_ff), lambda b, i: (layer, 0, 0)),
            _resident((1, d_ff, d), lambda b, i: (layer, 0, 0)),
        ],
        out_specs=pl.BlockSpec((1, tm, d), lambda b, i: (b, i, 0)),
        compiler_params=_params(("parallel", "parallel")),
        name="ffn",
    )(x, g, sh, sc, gate, w_in, w_out)


def _norm_rope(u, bd_ref, g_t, cos, sin, f32_ref, bf16_ref):
    ssq = _dot((u * u).astype(BF16), bd_ref[...])
    y = u * lax.rsqrt(ssq * (1.0 / HEAD_DIM) + EPS) * g_t
    tm = u.shape[0]
    lane = lax.broadcasted_iota(jnp.int32, (tm, LANES), 1)
    first_half = (lane & (HEAD_DIM - 1)) < HEAD_DIM // 2
    for h in range(u.shape[1] // LANES):
        yh = y[:, h * LANES:(h + 1) * LANES]
        partner = jnp.where(first_half,
                            pltpu.roll(yh, LANES - HEAD_DIM // 2, 1),
                            pltpu.roll(yh, HEAD_DIM // 2, 1))
        r = yh * cos + partner * sin
        if f32_ref is not None:
            f32_ref[0, :, h * LANES:(h + 1) * LANES] = r
        bf16_ref[0, :, h * LANES:(h + 1) * LANES] = r.astype(BF16)


def _inproj_kernel(x_ref, g_ref, sh_ref, sc_ref, w_ref, bd_ref, gq_ref, gk_ref, cos_ref, sin_ref,
                   z_ref, p_ref, k_ref, v_ref, qb_ref, kb_ref, vt_ref, *, conv_w, pool_w, attn_w):
    x = x_ref[0]
    h = _rms_mod(x, g_ref[...], sh_ref[0], sc_ref[0]).astype(BF16)
    o = 0
    a = _dot(h, w_ref[0, :, o:o + conv_w]); o += conv_w
    gt = _dot(h, w_ref[0, :, o:o + conv_w]); o += conv_w
    z_ref[0] = a * _sigmoid(gt)
    p_ref[0] = _dot(h, w_ref[0, :, o:o + pool_w]); o += pool_w
    cos = cos_ref[0]
    sin = sin_ref[0]
    uq = _dot(h, w_ref[0, :, o:o + attn_w]); o += attn_w
    _norm_rope(uq, bd_ref, gq_ref[...], cos, sin, None, qb_ref)
    uk = _dot(h, w_ref[0, :, o:o + attn_w]); o += attn_w
    _norm_rope(uk, bd_ref, gk_ref[...], cos, sin, k_ref, kb_ref)
    uv = _dot(h, w_ref[0, :, o:o + attn_w])
    v_ref[0] = uv
    for hd in range(attn_w // V_DIM):
        vt_ref[0, hd] = uv[:, hd * V_DIM:(hd + 1) * V_DIM].T.astype(BF16)


def _inproj(x, g, sh, sc, w, bd, gq_t, gk_t, cos, sin, layer, tm, conv_w, pool_w):
    nb, tb, d = x.shape
    attn_w = N_HEADS * V_DIM
    n = w.shape[2]
    row = lambda width, dt: jax.ShapeDtypeStruct((nb, tb, width), dt)
    rspec = lambda width: pl.BlockSpec((1, tm, width), lambda b, i: (b, i, 0))
    return pl.pallas_call(
        functools.partial(_inproj_kernel, conv_w=conv_w, pool_w=pool_w, attn_w=attn_w),
        out_shape=(row(conv_w, F32), row(pool_w, F32), row(attn_w, F32), row(attn_w, F32),
                   row(attn_w, BF16), row(attn_w, BF16),
                   jax.ShapeDtypeStruct((nb, N_HEADS, V_DIM, tb), BF16)),
        grid=(nb, tb // tm),
        in_specs=[
            rspec(d),
            pl.BlockSpec((1, d), lambda b, i: (0, 0)),
            _row_spec(sh, tm), _row_spec(sc, tm),
            _resident((1, d, n), lambda b, i: (layer, 0, 0)),
            _resident((attn_w, attn_w), lambda b, i: (0, 0)),
            pl.BlockSpec((1, attn_w), lambda b, i: (0, 0)),
            pl.BlockSpec((1, attn_w), lambda b, i: (0, 0)),
            pl.BlockSpec((1, tm, LANES), lambda b, i: (0, i, 0)),
            pl.BlockSpec((1, tm, LANES), lambda b, i: (0, i, 0)),
        ],
        out_specs=(rspec(conv_w), rspec(pool_w), rspec(attn_w), rspec(attn_w), rspec(attn_w), rspec(attn_w),
                   pl.BlockSpec((1, N_HEADS, V_DIM, tm), lambda b, i: (b, 0, 0, i))),
        compiler_params=_params(("parallel", "parallel")),
        name="inproj",
    )(x, g, sh, sc, w, bd, gq_t, gk_t, cos, sin)


def _convpool_kernel(z_ref, zh_ref, zi_ref, p_ref, ph_ref, pi_ref, wdw_ref, bdw_ref, lng_ref, lnb_ref,
                     wg_ref, ps_ref, yc_ref, mp_ref, zc_s, pc_s, *, tm, pos0):
    i = pl.program_id(1)
    first = i == 0
    zc_s[0:CONV_HALO, :] = jnp.where(first, zi_ref[0], zh_ref[0])
    zc_s[CONV_HALO:CONV_HALO + tm, :] = z_ref[0]
    pc_s[0:POOL_HALO, :] = jnp.where(first, pi_ref[0], ph_ref[0])
    pc_s[POOL_HALO:POOL_HALO + tm, :] = p_ref[0]

    width = zc_s.shape[1]
    rc = min(32, tm)
    for c in range(tm // rc):
        r0 = c * rc
        acc = jnp.broadcast_to(bdw_ref[...], (rc, width))
        for j in range(CONV_K):
            s = r0 + CONV_HALO - (CONV_K - 1) + j
            acc = acc + wdw_ref[j:j + 1, :] * zc_s[s:s + rc, :]
        mu = jnp.mean(acc, axis=-1, keepdims=True)
        dlt = acc - mu
        var = jnp.mean(dlt * dlt, axis=-1, keepdims=True)
        yn = dlt * lax.rsqrt(var + EPS) * lng_ref[...] + lnb_ref[...]
        yc_ref[0, r0:r0 + rc, :] = (yn * _sigmoid(yn)).astype(BF16)

    gw = width // len(POOL_WINDOWS)
    rp = min(128, tm)
    for c in range(tm // rp):
        r0 = c * rp
        pos = lax.broadcasted_iota(jnp.int32, (rp, 1), 0) + (pos0 + i * tm + r0)
        for g, w in enumerate(POOL_WINDOWS):
            lo = g * gw
            cur = pc_s[POOL_HALO + r0:POOL_HALO + r0 + rp, lo:lo + gw]
            acc = cur
            for s in range(1, w):
                acc = acc + pc_s[POOL_HALO + r0 - s:POOL_HALO + r0 - s + rp, lo:lo + gw]
            cnt = jnp.minimum(w, pos + 1).astype(F32)
            dlt = acc / cnt - cur
            m = _dot(dlt.astype(BF16), wg_ref[0, g]) * ps_ref[:, lo:lo + gw]
            mp_ref[0, r0:r0 + rp, lo:lo + gw] = m.astype(BF16)


def _convpool(z, z_halo_src, z_init, p, p_halo_src, p_init, w_dw, b_dw, ln_g, ln_b, w_grp, pool_scale,
              layer, tm, pos0):
    nb, tb, width = z.shape
    gw = width // len(POOL_WINDOWS)
    zsteps, psteps = tm // CONV_HALO, tm // POOL_HALO
    vec = pl.BlockSpec((1, width), lambda b, i: (0, 0))
    return pl.pallas_call(
        functools.partial(_convpool_kernel, tm=tm, pos0=pos0),
        out_shape=(jax.ShapeDtypeStruct((nb, tb, width), BF16), jax.ShapeDtypeStruct((nb, tb, width), BF16)),
        grid=(nb, tb // tm),
        in_specs=[
            pl.BlockSpec((1, tm, width), lambda b, i: (b, i, 0)),
            pl.BlockSpec((1, CONV_HALO, width), lambda b, i: (b, jnp.maximum(i * zsteps - 1, 0), 0)),
            pl.BlockSpec((1, CONV_HALO, width), lambda b, i: (b, 0, 0)),
            pl.BlockSpec((1, tm, width), lambda b, i: (b, i, 0)),
            pl.BlockSpec((1, POOL_HALO, width), lambda b, i: (b, jnp.maximum(i * psteps - 1, 0), 0)),
            pl.BlockSpec((1, POOL_HALO, width), lambda b, i: (b, 0, 0)),
            pl.BlockSpec((CONV_K, width), lambda b, i: (0, 0)),
            vec, vec, vec,
            pl.BlockSpec((1, len(POOL_WINDOWS), gw, gw), lambda b, i: (layer, 0, 0, 0)),
            vec,
        ],
        out_specs=(pl.BlockSpec((1, tm, width), lambda b, i: (b, i, 0)),
                   pl.BlockSpec((1, tm, width), lambda b, i: (b, i, 0))),
        scratch_shapes=[pltpu.VMEM((CONV_HALO + tm, width), F32), pltpu.VMEM((POOL_HALO + tm, width), F32)],
        compiler_params=_params(("parallel", "arbitrary")),
        name="convpool",
    )(z, z_halo_src, z_init, p, p_halo_src, p_init, w_dw, b_dw, ln_g, ln_b, w_grp, pool_scale)


def _attn_kernel(qi_tab, j_tab, q_ref, k_ref, vt_ref, gs_ref, lam_ref, o_ref, qbd_s, m_s, l_s, acc_s,
                 *, tq, tk, causal, kv_len, n_kv):
    p = pl.program_id(2)
    qi = qi_tab[p]
    j = j_tab[p]

    @pl.when(j == 0)
    def _init():
        qt = q_ref[0].astype(F32).T
        row = lax.broadcasted_iota(jnp.int32, qt.shape, 0)
        qbd_s[:, :tq] = jnp.where(row < HEAD_DIM, qt, 0.0).astype(BF16)
        qbd_s[:, tq:] = jnp.where(row >= HEAD_DIM, qt, 0.0).astype(BF16)
        m_s[...] = jnp.full(m_s.shape, -jnp.inf, F32)
        l_s[...] = jnp.zeros(l_s.shape, F32)
        acc_s[...] = jnp.zeros(acc_s.shape, F32)

    def step(masked):
        st = _dot(k_ref[0], qbd_s[...])
        if masked:
            kpos = j * tk + lax.broadcasted_iota(jnp.int32, st.shape, 0)
            if causal:
                col = lax.broadcasted_iota(jnp.int32, st.shape, 1)
                qpos = qi * tq + jnp.where(col >= tq, col - tq, col)
                shift = CHUNK.bit_length() - 1
                vis = lax.shift_right_logical(kpos, shift) <= lax.shift_right_logical(qpos, shift)
            else:
                vis = kpos < kv_len
            st = jnp.where(vis, st, -jnp.inf)
        m_prev = m_s[...]
        m_new = jnp.maximum(m_prev, jnp.max(st, axis=0, keepdims=True))
        alpha = jnp.exp(m_prev - m_new)
        pt = jnp.exp(st - m_new)
        l_s[...] = alpha * l_s[...] + jnp.sum(pt, axis=0, keepdims=True)
        acc_s[...] = acc_s[...] * alpha + _dot(vt_ref[0, 0], pt.astype(BF16))
        m_s[...] = m_new

    if causal:
        full = (j * tk + tk) <= (qi * tq + CHUNK)
        pl.when(full)(lambda: step(False))
        pl.when(jnp.logical_not(full))(lambda: step(True))
        last = j == (qi * tq + tq - 1) // tk
    else:
        if kv_len < n_kv * tk:
            pl.when(j < n_kv - 1)(lambda: step(False))
            pl.when(j == n_kv - 1)(lambda: step(True))
        else:
            step(False)
        last = j == n_kv - 1

    @pl.when(last)
    def _finish():
        l = l_s[...]
        acc = acc_s[...]
        ot = acc[:, :tq] / l[:, :tq] - lam_ref[0, 0] * (acc[:, tq:] / l[:, tq:])
        ms = jnp.mean(ot * ot, axis=0, keepdims=True)
        ot = ot * lax.rsqrt(ms + EPS) * gs_ref[...]
        o_ref[0] = ot.T.astype(BF16)


def _attn(qb, kb, vt, g_sub_col, lam, tq, tk, causal, kv_len):
    nb, t_q, width = qb.shape
    s_len = kb.shape[1]
    n_q, n_kv = t_q // tq, s_len // tk
    pairs = [(qi, j) for qi in range(n_q) for j in range(n_kv)
             if (not causal) or j * tk < qi * tq + tq]
    qi_tab = jnp.asarray(np.array([p[0] for p in pairs], np.int32))
    j_tab = jnp.asarray(np.array([p[1] for p in pairs], np.int32))
    grid_spec = pltpu.PrefetchScalarGridSpec(
        num_scalar_prefetch=2,
        grid=(nb, N_HEADS, len(pairs)),
        in_specs=[
            pl.BlockSpec((1, tq, V_DIM), lambda b, h, p, qt, jt: (b, qt[p], h)),
            pl.BlockSpec((1, tk, V_DIM), lambda b, h, p, qt, jt: (b, jt[p], h)),
            pl.BlockSpec((1, 1, V_DIM, tk), lambda b, h, p, qt, jt: (b, h, 0, jt[p])),
            pl.BlockSpec((V_DIM, 1), lambda b, h, p, qt, jt: (0, 0)),
            pl.BlockSpec(memory_space=pltpu.SMEM),
        ],
        out_specs=pl.BlockSpec((1, tq, V_DIM), lambda b, h, p, qt, jt: (b, qt[p], h)),
        scratch_shapes=[
            pltpu.VMEM((V_DIM, 2 * tq), BF16),
            pltpu.VMEM((1, 2 * tq), F32),
            pltpu.VMEM((1, 2 * tq), F32),
            pltpu.VMEM((V_DIM, 2 * tq), F32),
        ],
    )
    return pl.pallas_call(
        functools.partial(_attn_kernel, tq=tq, tk=tk, causal=causal, kv_len=kv_len, n_kv=n_kv),
        out_shape=jax.ShapeDtypeStruct((nb, t_q, width), BF16),
        grid_spec=grid_spec,
        compiler_params=_params(("parallel", "parallel", "arbitrary")),
        name="attn",
    )(qi_tab, j_tab, qb, kb, vt, g_sub_col, lam)


def _merge_kernel(x_ref, yc_ref, mp_ref, o_ref, g_ref, sh_ref, sc_ref, g2_ref,
                  wg_ref, wc_ref, wp_ref, wa_ref, wo_ref, out_ref):
    x = x_ref[0]
    d = x.shape[1]
    h = _rms_mod(x, g_ref[...], sh_ref[0], sc_ref[0]).astype(BF16)
    merged = _sigmoid(_dot(h, wg_ref[0, :, 0:d])) * _dot(yc_ref[0], wc_ref[0])
    merged = merged + _sigmoid(_dot(h, wg_ref[0, :, d:2 * d])) * _dot(mp_ref[0], wp_ref[0])
    merged = merged + _sigmoid(_dot(h, wg_ref[0, :, 2 * d:3 * d])) * _dot(o_ref[0], wa_ref[0])
    out_ref[0] = x + g2_ref[0] * _dot(merged.astype(BF16), wo_ref[0])


def _merge(x, yc, mp, o, g, sh, sc, g2, w_gate, w_conv_out, w_pool_out, w_attn_out, w_out, layer, tm):
    nb, tb, d = x.shape
    rspec = lambda width: pl.BlockSpec((1, tm, width), lambda b, i: (b, i, 0))
    wspec = lambda w: _resident((1,) + w.shape[1:], lambda b, i: (layer, 0, 0))
    return pl.pallas_call(
        _merge_kernel,
        out_shape=jax.ShapeDtypeStruct(x.shape, F32),
        grid=(nb, tb // tm),
        in_specs=[
            rspec(d), rspec(yc.shape[2]), rspec(mp.shape[2]), rspec(o.shape[2]),
            pl.BlockSpec((1, d), lambda b, i: (0, 0)),
            _row_spec(sh, tm), _row_spec(sc, tm), _row_spec(g2, tm),
            wspec(w_gate), wspec(w_conv_out), wspec(w_pool_out), wspec(w_attn_out), wspec(w_out),
        ],
        out_specs=rspec(d),
        compiler_params=_params(("parallel", "parallel")),
        name="merge",
    )(x, yc, mp, o, g, sh, sc, g2, w_gate, w_conv_out, w_pool_out, w_attn_out, w_out)


def _rope_tables(pos):
    half = HEAD_DIM // 2
    inv = ROPE_THETA ** (-jnp.arange(half, dtype=F32) / half)
    ang = pos.astype(F32)[:, None] * inv[None, :]
    cos, sin = jnp.cos(ang), jnp.sin(ang)
    return jnp.tile(cos, (1, 4)), jnp.concatenate([-sin, sin, -sin, sin], axis=-1)


def _tile_rows(total, want):
    t = min(want, total)
    assert total % t == 0, (total, t)
    return t


def kernel(x_prompt, x_sample, cache_attn_k, cache_attn_v, state_conv, state_pool, c_prompt, c_sample, w_ada, b_ada, g_ffn1, w_ffn1_in, w_ffn1_out, g_mix, w_in, w_dw, b_dw, ln_conv_g, ln_conv_b, w_conv_out, w_pool_grp, pool_scale, w_pool_out, g_q, g_k, lam_q1, lam_k1, lam_q2, lam_k2, g_sub, w_attn_out, w_out, g_ffn2, w_ffn2_in, w_ffn2_out):
    bp, tp, d = x_prompt.shape
    bs, ts, _ = x_sample.shape
    n_layers = w_ada.shape[0]
    past = cache_attn_k.shape[2]
    conv_w = w_dw.shape[2]
    pool_w = pool_scale.shape[1]
    attn_w = N_HEADS * V_DIM
    proj_w = 2 * conv_w + pool_w + 3 * attn_w
    assert CONV_HALO >= CONV_K - 1 and POOL_HALO >= POOL_BUF and tp >= CONV_HALO

    w_ffn1_in_b, w_ffn1_out_b = w_ffn1_in.astype(BF16), w_ffn1_out.astype(BF16)
    w_ffn2_in_b, w_ffn2_out_b = w_ffn2_in.astype(BF16), w_ffn2_out.astype(BF16)
    w_proj_b = w_in[:, :, :proj_w].astype(BF16)
    w_gate_b = w_in[:, :, proj_w:].astype(BF16)
    w_conv_out_b, w_pool_out_b = w_conv_out.astype(BF16), w_pool_out.astype(BF16)
    w_attn_out_b, w_out_b = w_attn_out.astype(BF16), w_out.astype(BF16)
    w_grp_b = w_pool_grp.astype(BF16)
    chunk_id = np.arange(attn_w) // HEAD_DIM
    bd = jnp.asarray(chunk_id[:, None] == chunk_id[None, :], BF16)

    rows = bp + bs
    rows_pad = -(-rows // 16) * 16
    c_all = jnp.concatenate([c_prompt, c_sample, jnp.zeros((rows_pad - rows, d), F32)], axis=0)
    mod = _ada(c_all, w_ada, b_ada).reshape(n_layers, rows_pad, N_MOD, d)

    cos_p, sin_p = _rope_tables(jnp.arange(tp))
    cos_s, sin_s = _rope_tables(past + jnp.arange(ts))
    cos_p, sin_p = cos_p[None], sin_p[None]
    cos_s, sin_s = jnp.tile(cos_s, (bs, 1))[None], jnp.tile(sin_s, (bs, 1))[None]

    ms = bs * ts
    tq_s = LANES
    pad_s = 512 - ts
    s_pad = past + ts + pad_s
    kc = cache_attn_k.reshape(n_layers, bs, past, attn_w).astype(BF16)
    vtc = jnp.transpose(cache_attn_v, (0, 1, 3, 4, 2)).astype(BF16)

    tm_p = _tile_rows(tp, 256)
    tm_cp = _tile_rows(tp, 512)
    tq_p = _tile_rows(tp, 512)
    zeros_conv = jnp.zeros((bp, CONV_HALO, conv_w), F32)
    zeros_pool = jnp.zeros((bp, POOL_HALO, pool_w), F32)

    xp = x_prompt
    xs = x_sample.reshape(1, ms, d)
    outs = [[] for _ in range(8)]
    for l in range(n_layers):
        lam_init = 0.8 - 0.6 * math.exp(-0.3 * l)
        lam = (jnp.exp(jnp.sum(lam_q1[l] * lam_k1[l])) - jnp.exp(jnp.sum(lam_q2[l] * lam_k2[l])) + lam_init)
        lam = lam.reshape(1, 1).astype(F32)
        g_sub_col = (g_sub[l] * (1.0 - lam_init)).reshape(V_DIM, 1)
        gq_t = (jnp.tile(g_q[l], attn_w // HEAD_DIM) * HEAD_DIM ** -0.5)[None]
        gk_t = jnp.tile(g_k[l], attn_w // HEAD_DIM)[None]

        mod_p = [mod[l, :bp, k][:, None, :] for k in range(N_MOD)]
        mod_s = [jnp.repeat(mod[l, bp:rows, k], ts, axis=0)[None] for k in range(N_MOD)]

        def layer(x, m, tm, tm_c, cos, sin, conv_state, pool_state, pos0, attend):
            sh1, sc1, g1, sh2, sc2, g2, sh3, sc3, g3 = m
            x = _ffn(x, g_ffn1[l][None], sh1, sc1, g1, w_ffn1_in_b, w_ffn1_out_b, l, tm)
            z, p, k, v, qb, kb, vt = _inproj(x, g_mix[l][None], sh2, sc2, w_proj_b, bd, gq_t, gk_t,
                                             cos, sin, l, tm, conv_w, pool_w)
            nb = conv_state.shape[0]
            zb = z.reshape(nb, -1, conv_w)
            pb = p.reshape(nb, -1, pool_w)
            z_init = jnp.pad(conv_state, ((0, 0), (CONV_HALO - (CONV_K - 1), 0), (0, 0)))
            p_init = jnp.pad(pool_state, ((0, 0), (POOL_HALO - POOL_BUF, 0), (0, 0)))
            halo_ok = zb.shape[1] >= CONV_HALO
            yc, mp = _convpool(zb, zb if halo_ok else z_init, z_init, pb, pb if halo_ok else p_init, p_init,
                               w_dw[l], b_dw[l][None], ln_conv_g[l][None], ln_conv_b[l][None],
                               w_grp_b, pool_scale[l][None], l, tm_c, pos0)
            o = attend(qb, kb, vt)
            x = _merge(x, yc.reshape(x.shape[0], -1, conv_w), mp.reshape(x.shape[0], -1, pool_w), o,
                       g_mix[l][None], sh2, sc2, g2, w_gate_b, w_conv_out_b, w_pool_out_b, w_attn_out_b,
                       w_out_b, l, tm)
            x = _ffn(x, g_ffn2[l][None], sh3, sc3, g3, w_ffn2_in_b, w_ffn2_out_b, l, tm)
            new_conv = jnp.concatenate([conv_state, zb], axis=1)[:, -(CONV_K - 1):]
            new_pool = jnp.concatenate([pool_state, pb], axis=1)[:, -POOL_BUF:]
            return x, k, v, new_conv, new_pool

        def attend_p(qb, kb, vt):
            return _attn(qb, kb, vt, g_sub_col, lam, tq_p, tq_p, True, tp)

        def attend_s(qb, kb, vt):
            q = jnp.pad(qb.reshape(bs, ts, attn_w), ((0, 0), (0, tq_s - ts), (0, 0)))
            k_all = jnp.concatenate([kc[l], kb.reshape(bs, ts, attn_w),
                                     jnp.zeros((bs, pad_s, attn_w), BF16)], axis=1)
            vt_new = jnp.transpose(vt.reshape(N_HEADS, V_DIM, bs, ts), (2, 0, 1, 3))
            vt_all = jnp.concatenate([vtc[l], vt_new, jnp.zeros((bs, N_HEADS, V_DIM, pad_s), BF16)], axis=3)
            o = _attn(q, k_all, vt_all, g_sub_col, lam, tq_s, s_pad, False, past + ts)
            return o[:, :ts].reshape(1, ms, attn_w)

        xp, kp, vp, cp, pp = layer(xp, mod_p, tm_p, tm_cp, cos_p, sin_p,
                                   jnp.zeros((bp, CONV_K - 1, conv_w), F32),
                                   jnp.zeros((bp, POOL_BUF, pool_w), F32), 0, attend_p)
        xs, kn, vn, cn, pn = layer(xs, mod_s, ms, ts, cos_s, sin_s, state_conv[l], state_pool[l], past, attend_s)
        for lst, val in zip(outs, (kp, vp, cp, pp, kn, vn, cn, pn)):
            lst.append(val)

    kp, vp, cp, pp, kn, vn, cn, pn = [jnp.stack(o) for o in outs]
    return (xp, xs.reshape(bs, ts, d),
            kp.reshape(n_layers, bp, tp, N_HEADS, 2, HEAD_DIM), vp.reshape(n_layers, bp, tp, N_HEADS, V_DIM),
            cp, pp,
            kn.reshape(n_layers, bs, ts, N_HEADS, 2, HEAD_DIM), vn.reshape(n_layers, bs, ts, N_HEADS, V_DIM),
            cn, pn)
```

```python
import functools
import math

import numpy as np
import jax
import jax.numpy as jnp
from jax import lax
from jax.experimental import pallas as pl
from jax.experimental.pallas import tpu as pltpu

F32 = jnp.float32
BF16 = jnp.bfloat16

EPS = 1e-6
CHUNK = 64
N_HEADS = 8
HEAD_DIM = 64
V_DIM = 2 * HEAD_DIM
CONV_K = 31
POOL_WINDOWS = (2, 4, 8, 16)
POOL_BUF = 15
ROPE_THETA = 10000.0
N_MOD = 9

SUBLANES = 8
LANES = 128
CONV_HALO = 32
POOL_HALO = 16
ATT_KB = 256
V_ROWS = V_DIM + 16
V7X_VMEM_LIMIT = 56 * 1024 * 1024


def _params(semantics):
    return pltpu.CompilerParams(dimension_semantics=semantics, vmem_limit_bytes=V7X_VMEM_LIMIT)


def _dot(a, b):
    return jnp.dot(a, b, preferred_element_type=F32)


def _dot_nt(a, b):
    return lax.dot_general(a, b, (((1,), (1,)), ((), ())), preferred_element_type=F32)


def _sigmoid(x):
    return 1.0 / (1.0 + jnp.exp(-x))


def _rms_mod(x, g, sh, sc):
    ms = jnp.mean(x * x, axis=-1, keepdims=True)
    y = x * lax.rsqrt(ms + EPS) * g
    return y * (1.0 + sc) + sh


def _row_spec(arr, tm):
    d = arr.shape[-1]
    if arr.shape[1] == 1:
        return pl.BlockSpec((1, 1, d), lambda b, i: (b, 0, 0))
    return pl.BlockSpec((1, tm, d), lambda b, i: (b, i, 0))


def _resident(shape, index_map):
    return pl.BlockSpec(shape, index_map, pipeline_mode=pl.Buffered(1))


def _ada_kernel(c_ref, w_ref, b_ref, o_ref):
    c = c_ref[...]
    a = (c * _sigmoid(c)).astype(BF16)
    o_ref[0] = _dot(a, w_ref[0].astype(BF16)) + b_ref[0]


def _ada(c_all, w_ada, b_ada):
    n_layers, d, n = w_ada.shape
    rows = c_all.shape[0]
    tn = 1024
    return pl.pallas_call(
        _ada_kernel,
        out_shape=jax.ShapeDtypeStruct((n_layers, rows, n), F32),
        grid=(n_layers, n // tn),
        in_specs=[
            pl.BlockSpec((rows, d), lambda l, j: (0, 0)),
            pl.BlockSpec((1, d, tn), lambda l, j: (l, 0, j)),
            pl.BlockSpec((1, 1, tn), lambda l, j: (l, 0, j)),
        ],
        out_specs=pl.BlockSpec((1, rows, tn), lambda l, j: (l, 0, j)),
        compiler_params=_params(("parallel", "parallel")),
        name="ada",
    )(c_all, w_ada, b_ada.reshape(n_layers, 1, n))


def _ffn_kernel(x_ref, g_ref, sh_ref, sc_ref, gt_ref, win_ref, wout_ref, o_ref, *, d_ff):
    x = x_ref[0]
    h = _rms_mod(x, g_ref[...], sh_ref[0], sc_ref[0]).astype(BF16)
    a = _dot(h, win_ref[0, :, :d_ff])
    b = _dot(h, win_ref[0, :, d_ff:])
    hid = (a * _sigmoid(a) * b).astype(BF16)
    y = _dot(hid, wout_ref[0])
    o_ref[0] = x + 0.5 * gt_ref[0] * y


def _ffn(x, g, sh, sc, gate, w_in, w_out, layer, tm):
    nb, tb, d = x.shape
    d_ff = w_out.shape[1]
    return pl.pallas_call(
        functools.partial(_ffn_kernel, d_ff=d_ff),
        out_shape=jax.ShapeDtypeStruct(x.shape, F32),
        grid=(nb, tb // tm),
        in_specs=[
            pl.BlockSpec((1, tm, d), lambda b, i: (b, i, 0)),
            pl.BlockSpec((1, d), lambda b, i: (0, 0)),
            _row_spec(sh, tm), _row_spec(sc, tm), _row_spec(gate, tm),
            _resident((1, d, 2 * d_ff), lambda b, i: (layer, 0, 0)),
            _resident((1, d_ff, d), lambda b, i: (layer, 0, 0)),
        ],
        out_specs=pl.BlockSpec((1, tm, d), lambda b, i: (b, i, 0)),
        compiler_params=_params(("parallel", "parallel")),
        name="ffn",
    )(x, g, sh, sc, gate, w_in, w_out)


def _norm_rope(u, bd_ref, g_t, cos, sin, f32_ref, bf16_ref):
    ssq = _dot((u * u).astype(BF16), bd_ref[...])
    y = u * lax.rsqrt(ssq * (1.0 / HEAD_DIM) + EPS) * g_t
    tm = u.shape[0]
    lane = lax.broadcasted_iota(jnp.int32, (tm, LANES), 1)
    first_half = (lane & (HEAD_DIM - 1)) < HEAD_DIM // 2
    for h in range(u.shape[1] // LANES):
        yh = y[:, h * LANES:(h + 1) * LANES]
        partner = jnp.where(first_half,
                            pltpu.roll(yh, LANES - HEAD_DIM // 2, 1),
                            pltpu.roll(yh, HEAD_DIM // 2, 1))
        r = yh * cos + partner * sin
        if f32_ref is not None:
            f32_ref[0, :, h * LANES:(h + 1) * LANES] = r
        bf16_ref[0, :, h * LANES:(h + 1) * LANES] = r.astype(BF16)


def _inproj_kernel(x_ref, g_ref, sh_ref, sc_ref, w_ref, bd_ref, gq_ref, gk_ref, cos_ref, sin_ref,
                   z_ref, p_ref, k_ref, v_ref, qb_ref, kb_ref, vt_ref, *, conv_w, pool_w, attn_w):
    x = x_ref[0]
    h = _rms_mod(x, g_ref[...], sh_ref[0], sc_ref[0]).astype(BF16)
    o = 0
    a = _dot(h, w_ref[0, :, o:o + conv_w]); o += conv_w
    gt = _dot(h, w_ref[0, :, o:o + conv_w]); o += conv_w
    z_ref[0] = a * _sigmoid(gt)
    p_ref[0] = _dot(h, w_ref[0, :, o:o + pool_w]); o += pool_w
    cos = cos_ref[0]
    sin = sin_ref[0]
    uq = _dot(h, w_ref[0, :, o:o + attn_w]); o += attn_w
    _norm_rope(uq, bd_ref, gq_ref[...], cos, sin, None, qb_ref)
    uk = _dot(h, w_ref[0, :, o:o + attn_w]); o += attn_w
    _norm_rope(uk, bd_ref, gk_ref[...], cos, sin, k_ref, kb_ref)
    uv = _dot(h, w_ref[0, :, o:o + attn_w])
    v_ref[0] = uv
    tail = lax.broadcasted_iota(jnp.int32, (V_ROWS - V_DIM, ATT_KB), 0)
    ones_rows = jnp.where(tail == 0, 1.0, 0.0).astype(BF16)
    for hd in range(attn_w // V_DIM):
        for c in range(uv.shape[0] // ATT_KB):
            blk = uv[c * ATT_KB:(c + 1) * ATT_KB, hd * V_DIM:(hd + 1) * V_DIM]
            vt_ref[0, hd, c, 0:V_DIM, :] = blk.T.astype(BF16)
            vt_ref[0, hd, c, V_DIM:V_ROWS, :] = ones_rows


def _inproj(x, g, sh, sc, w, bd, gq_t, gk_t, cos, sin, layer, tm, conv_w, pool_w):
    nb, tb, d = x.shape
    attn_w = N_HEADS * V_DIM
    n = w.shape[2]
    assert tm % ATT_KB == 0
    row = lambda width, dt: jax.ShapeDtypeStruct((nb, tb, width), dt)
    rspec = lambda width: pl.BlockSpec((1, tm, width), lambda b, i: (b, i, 0))
    return pl.pallas_call(
        functools.partial(_inproj_kernel, conv_w=conv_w, pool_w=pool_w, attn_w=attn_w),
        out_shape=(row(conv_w, F32), row(pool_w, F32), row(attn_w, F32), row(attn_w, F32),
                   row(attn_w, BF16), row(attn_w, BF16),
                   jax.ShapeDtypeStruct((nb, N_HEADS, tb // ATT_KB, V_ROWS, ATT_KB), BF16)),
        grid=(nb, tb // tm),
        in_specs=[
            rspec(d),
            pl.BlockSpec((1, d), lambda b, i: (0, 0)),
            _row_spec(sh, tm), _row_spec(sc, tm),
            _resident((1, d, n), lambda b, i: (layer, 0, 0)),
            _resident((attn_w, attn_w), lambda b, i: (0, 0)),
            pl.BlockSpec((1, attn_w), lambda b, i: (0, 0)),
            pl.BlockSpec((1, attn_w), lambda b, i: (0, 0)),
            pl.BlockSpec((1, tm, LANES), lambda b, i: (0, i, 0)),
            pl.BlockSpec((1, tm, LANES), lambda b, i: (0, i, 0)),
        ],
        out_specs=(rspec(conv_w), rspec(pool_w), rspec(attn_w), rspec(attn_w), rspec(attn_w), rspec(attn_w),
                   pl.BlockSpec((1, N_HEADS, tm // ATT_KB, V_ROWS, ATT_KB), lambda b, i: (b, 0, i, 0, 0))),
        compiler_params=_params(("parallel", "parallel")),
        name="inproj",
    )(x, g, sh, sc, w, bd, gq_t, gk_t, cos, sin)


def _convpool_kernel(z_ref, zh_ref, zi_ref, p_ref, ph_ref, pi_ref, wdw_ref, bdw_ref, lng_ref, lnb_ref,
                     wg_ref, ps_ref, yc_ref, mp_ref, zc_s, zsh_s, pc_s, *, tm, pos0):
    i = pl.program_id(1)
    first = i == 0
    zc_s[0:CONV_HALO, :] = jnp.where(first, zi_ref[0], zh_ref[0])
    zc_s[CONV_HALO:CONV_HALO + tm, :] = z_ref[0]
    pc_s[0:POOL_HALO, :] = jnp.where(first, pi_ref[0], ph_ref[0])
    pc_s[POOL_HALO:POOL_HALO + tm, :] = p_ref[0]
    n_sh = zsh_s.shape[1]
    for k in range(1, SUBLANES):
        zsh_s[k] = zc_s[k:k + n_sh, :]

    width = zc_s.shape[1]
    rc = min(32, tm)
    for c in range(tm // rc):
        r0 = c * rc
        acc = jnp.broadcast_to(bdw_ref[...], (rc, width))
        for j in range(CONV_K):
            off = r0 + CONV_HALO - (CONV_K - 1) + j
            k, base = off % SUBLANES, off - off % SUBLANES
            tap = zc_s[base:base + rc, :] if k == 0 else zsh_s[k, base:base + rc, :]
            acc = acc + wdw_ref[j:j + 1, :] * tap
        mu = jnp.mean(acc, axis=-1, keepdims=True)
        dlt = acc - mu
        var = jnp.mean(dlt * dlt, axis=-1, keepdims=True)
        yn = dlt * lax.rsqrt(var + EPS) * lng_ref[...] + lnb_ref[...]
        yc_ref[0, r0:r0 + rc, :] = (yn * _sigmoid(yn)).astype(BF16)

    gw = width // len(POOL_WINDOWS)
    rp = min(128, tm)
    for c in range(tm // rp):
        r0 = c * rp
        pos = lax.broadcasted_iota(jnp.int32, (rp, 1), 0) + (pos0 + i * tm + r0)
        for g, w in enumerate(POOL_WINDOWS):
            lo = g * gw
            cur = pc_s[POOL_HALO + r0:POOL_HALO + r0 + rp, lo:lo + gw]
            acc = cur
            for s in range(1, w):
                acc = acc + pc_s[POOL_HALO + r0 - s:POOL_HALO + r0 - s + rp, lo:lo + gw]
            cnt = jnp.minimum(w, pos + 1).astype(F32)
            dlt = acc / cnt - cur
            m = _dot(dlt.astype(BF16), wg_ref[0, g]) * ps_ref[:, lo:lo + gw]
            mp_ref[0, r0:r0 + rp, lo:lo + gw] = m.astype(BF16)


def _convpool(z, z_halo_src, z_init, p, p_halo_src, p_init, w_dw, b_dw, ln_g, ln_b, w_grp, pool_scale,
              layer, tm, pos0):
    nb, tb, width = z.shape
    gw = width // len(POOL_WINDOWS)
    zsteps, psteps = tm // CONV_HALO, tm // POOL_HALO
    vec = pl.BlockSpec((1, width), lambda b, i: (0, 0))
    return pl.pallas_call(
        functools.partial(_convpool_kernel, tm=tm, pos0=pos0),
        out_shape=(jax.ShapeDtypeStruct((nb, tb, width), BF16), jax.ShapeDtypeStruct((nb, tb, width), BF16)),
        grid=(nb, tb // tm),
        in_specs=[
            pl.BlockSpec((1, tm, width), lambda b, i: (b, i, 0)),
            pl.BlockSpec((1, CONV_HALO, width), lambda b, i: (b, jnp.maximum(i * zsteps - 1, 0), 0)),
            pl.BlockSpec((1, CONV_HALO, width), lambda b, i: (b, 0, 0)),
            pl.BlockSpec((1, tm, width), lambda b, i: (b, i, 0)),
            pl.BlockSpec((1, POOL_HALO, width), lambda b, i: (b, jnp.maximum(i * psteps - 1, 0), 0)),
            pl.BlockSpec((1, POOL_HALO, width), lambda b, i: (b, 0, 0)),
            pl.BlockSpec((CONV_K, width), lambda b, i: (0, 0)),
            vec, vec, vec,
            pl.BlockSpec((1, len(POOL_WINDOWS), gw, gw), lambda b, i: (layer, 0, 0, 0)),
            vec,
        ],
        out_specs=(pl.BlockSpec((1, tm, width), lambda b, i: (b, i, 0)),
                   pl.BlockSpec((1, tm, width), lambda b, i: (b, i, 0))),
        scratch_shapes=[pltpu.VMEM((CONV_HALO + tm, width), F32),
                        pltpu.VMEM((SUBLANES, CONV_HALO + tm - SUBLANES, width), F32),
                        pltpu.VMEM((POOL_HALO + tm, width), F32)],
        compiler_params=_params(("parallel", "arbitrary")),
        name="convpool",
    )(z, z_halo_src, z_init, p, p_halo_src, p_init, w_dw, b_dw, ln_g, ln_b, w_grp, pool_scale)


def _attn_kernel(q_ref, k_ref, vt_ref, gs_ref, lam_ref, o_ref, qbd_s, s_s, m_s, acc_s, *, tq):
    kb = ATT_KB
    gwid = 2 * kb
    ng = tq // kb
    assert ng % 2 == 0
    qi = pl.program_id(2)

    qt = q_ref[0].astype(F32).T
    row = lax.broadcasted_iota(jnp.int32, (V_DIM, kb), 0)
    for c in range(ng):
        blk = qt[:, c * kb:(c + 1) * kb]
        qbd_s[:, c * gwid:c * gwid + kb] = jnp.where(row < HEAD_DIM, blk, 0.0).astype(BF16)
        qbd_s[:, c * gwid + kb:(c + 1) * gwid] = jnp.where(row >= HEAD_DIM, blk, 0.0).astype(BF16)
    m_s[...] = jnp.full(m_s.shape, -jnp.inf, F32)
    acc_s[...] = jnp.zeros(acc_s.shape, F32)

    def scores(jb, lo):
        keys = k_ref[0, pl.ds(pl.multiple_of(jb * kb, kb), kb), :]
        return _dot(keys, qbd_s[:, lo:])

    def update(st, jb, lo):
        m_prev = m_s[:, lo:]
        m_new = jnp.maximum(m_prev, jnp.max(st, axis=0, keepdims=True))
        alpha = jnp.exp2(m_prev - m_new)
        pt = jnp.exp2(st - m_new).astype(BF16)
        acc_s[:, lo:] = acc_s[:, lo:] * alpha + _dot(vt_ref[0, 0, jb], pt)
        m_s[:, lo:] = m_new

    n_full = ng * qi
    s_s[0] = scores(0, 0)

    def pair(jj, carry):
        j0 = 2 * jj
        s_s[1] = scores(j0 + 1, 0)
        update(s_s[0], j0, 0)
        s_s[0] = scores(j0 + 2, 0)
        update(s_s[1], j0 + 1, 0)
        return carry

    lax.fori_loop(0, (ng // 2) * qi, pair, 0)

    krow = lax.broadcasted_iota(jnp.int32, (kb, kb), 0)
    qcol = lax.broadcasted_iota(jnp.int32, (kb, kb), 1)
    shift = CHUNK.bit_length() - 1
    vis = lax.shift_right_logical(krow, shift) <= lax.shift_right_logical(qcol, shift)

    def masked(st):
        parts = [jnp.where(vis, st[:, :kb], -jnp.inf), jnp.where(vis, st[:, kb:gwid], -jnp.inf)]
        if st.shape[1] > gwid:
            parts.append(st[:, gwid:])
        return jnp.concatenate(parts, axis=1)

    for s in range(ng):
        lo = s * gwid
        if s + 1 < ng:
            s_s[(s + 1) % 2, :, lo + gwid:] = scores(n_full + s + 1, lo + gwid)
        update(masked(s_s[s % 2, :, lo:]), n_full + s, lo)

    lam = lam_ref[0, 0]
    for c in range(ng):
        lo = c * gwid
        o1 = acc_s[0:V_DIM, lo:lo + kb] / acc_s[V_DIM:V_DIM + 1, lo:lo + kb]
        o2 = acc_s[0:V_DIM, lo + kb:lo + gwid] / acc_s[V_DIM:V_DIM + 1, lo + kb:lo + gwid]
        ot = o1 - lam * o2
        ms = jnp.mean(ot * ot, axis=0, keepdims=True)
        ot = ot * lax.rsqrt(ms + EPS) * gs_ref[...]
        o_ref[0, c * kb:(c + 1) * kb, :] = ot.T.astype(BF16)


def _attn(qb, kb, vt, g_sub_col, lam):
    nb, t, width = qb.shape
    tq = min(4 * ATT_KB, t)
    assert t % tq == 0
    return pl.pallas_call(
        functools.partial(_attn_kernel, tq=tq),
        out_shape=jax.ShapeDtypeStruct((nb, t, width), BF16),
        grid=(nb, N_HEADS, t // tq),
        in_specs=[
            pl.BlockSpec((1, tq, V_DIM), lambda b, h, i: (b, i, h)),
            pl.BlockSpec((1, t, V_DIM), lambda b, h, i: (b, 0, h)),
            pl.BlockSpec((1, 1, t // ATT_KB, V_ROWS, ATT_KB), lambda b, h, i: (b, h, 0, 0, 0)),
            pl.BlockSpec((V_DIM, 1), lambda b, h, i: (0, 0)),
            pl.BlockSpec(memory_space=pltpu.SMEM),
        ],
        out_specs=pl.BlockSpec((1, tq, V_DIM), lambda b, h, i: (b, i, h)),
        scratch_shapes=[
            pltpu.VMEM((V_DIM, 2 * tq), BF16),
            pltpu.VMEM((2, ATT_KB, 2 * tq), F32),
            pltpu.VMEM((1, 2 * tq), F32),
            pltpu.VMEM((V_ROWS, 2 * tq), F32),
        ],
        compiler_params=_params(("parallel", "parallel", "arbitrary")),
        name="attn",
    )(qb, kb, vt, g_sub_col, lam)


def _attn_s_kernel(q_ref, kn_ref, vn_ref, kc_ref, vc_ref, gs_ref, lam_ref, o_ref, qbd_s, m_s, l_s, acc_s,
                   *, ts, n_kv):
    j = pl.program_id(1)
    width = q_ref.shape[2]

    def fold(s, v_bf16, m_prev, l_prev, acc_prev):
        m_new = jnp.maximum(m_prev, jnp.max(s, axis=1, keepdims=True))
        alpha = jnp.exp2(m_prev - m_new)
        p = jnp.exp2(s - m_new)
        l_s[...] = alpha * l_prev + jnp.sum(p, axis=1, keepdims=True)
        acc_s[...] = alpha * acc_prev + _dot(p.astype(BF16), v_bf16)
        m_s[...] = m_new

    @pl.when(j == 0)
    def _init():
        q = q_ref[0].astype(F32)
        col = lax.broadcasted_iota(jnp.int32, (ts, width), 1)
        for r in range(N_HEADS * 2):
            mine = (col >= r * HEAD_DIM) & (col < (r + 1) * HEAD_DIM)
            qbd_s[r * ts:(r + 1) * ts, :] = jnp.where(mine, q, 0.0).astype(BF16)
        s = _dot_nt(qbd_s[...], kn_ref[0])
        rows = s.shape[0]
        s = jnp.where(lax.broadcasted_iota(jnp.int32, s.shape, 1) < ts, s, -jnp.inf)
        fold(s, vn_ref[0], jnp.full((rows, 1), -jnp.inf, F32), jnp.zeros((rows, 1), F32),
             jnp.zeros((rows, width), F32))

    s = _dot_nt(qbd_s[...], kc_ref[0, 0].astype(BF16))
    fold(s, vc_ref[0, 0].astype(BF16), m_s[...], l_s[...], acc_s[...])

    @pl.when(j == n_kv - 1)
    def _finish():
        lam = lam_ref[0, 0]
        for h in range(N_HEADS):
            r0 = h * 2 * ts
            c0 = h * V_DIM
            o1 = acc_s[r0:r0 + ts, c0:c0 + V_DIM] / l_s[r0:r0 + ts, :]
            o2 = acc_s[r0 + ts:r0 + 2 * ts, c0:c0 + V_DIM] / l_s[r0 + ts:r0 + 2 * ts, :]
            o = o1 - lam * o2
            ms = jnp.mean(o * o, axis=-1, keepdims=True)
            o_ref[0, :, c0:c0 + V_DIM] = (o * lax.rsqrt(ms + EPS) * gs_ref[...]).astype(BF16)


def _attn_s(qb, kb_new, v_new, cache_k, cache_v, g_sub_row, lam, layer, tk):
    nb, ts, width = qb.shape
    past = cache_k.shape[2]
    n_kv = past // tk
    rows = N_HEADS * 2 * ts
    pad = ((0, 0), (0, LANES - ts), (0, 0))
    kb_new = jnp.pad(kb_new, pad)
    v_new = jnp.pad(v_new.astype(BF16), pad)
    new = pl.BlockSpec((1, ts, width), lambda b, j: (b, 0, 0))
    new_kv = pl.BlockSpec((1, LANES, width), lambda b, j: (b, 0, 0))
    old = pl.BlockSpec((1, 1, tk, width), lambda b, j: (layer, b, j, 0))
    return pl.pallas_call(
        functools.partial(_attn_s_kernel, ts=ts, n_kv=n_kv),
        out_shape=jax.ShapeDtypeStruct((nb, ts, width), BF16),
        grid=(nb, n_kv),
        in_specs=[new, new_kv, new_kv, old, old,
                  pl.BlockSpec((1, V_DIM), lambda b, j: (0, 0)),
                  pl.BlockSpec(memory_space=pltpu.SMEM)],
        out_specs=new,
        scratch_shapes=[
            pltpu.VMEM((rows, width), BF16),
            pltpu.VMEM((rows, 1), F32),
            pltpu.VMEM((rows, 1), F32),
            pltpu.VMEM((rows, width), F32),
        ],
        compiler_params=_params(("parallel", "arbitrary")),
        name="attn_s",
    )(qb, kb_new, v_new, cache_k, cache_v, g_sub_row, lam)


def _merge_kernel(x_ref, yc_ref, mp_ref, o_ref, g_ref, sh_ref, sc_ref, g2_ref,
                  wg_ref, wc_ref, wp_ref, wa_ref, wo_ref, out_ref):
    x = x_ref[0]
    d = x.shape[1]
    h = _rms_mod(x, g_ref[...], sh_ref[0], sc_ref[0]).astype(BF16)
    merged = _sigmoid(_dot(h, wg_ref[0, :, 0:d])) * _dot(yc_ref[0], wc_ref[0])
    merged = merged + _sigmoid(_dot(h, wg_ref[0, :, d:2 * d])) * _dot(mp_ref[0], wp_ref[0])
    merged = merged + _sigmoid(_dot(h, wg_ref[0, :, 2 * d:3 * d])) * _dot(o_ref[0], wa_ref[0])
    out_ref[0] = x + g2_ref[0] * _dot(merged.astype(BF16), wo_ref[0])


def _merge(x, yc, mp, o, g, sh, sc, g2, w_gate, w_conv_out, w_pool_out, w_attn_out, w_out, layer, tm):
    nb, tb, d = x.shape
    rspec = lambda width: pl.BlockSpec((1, tm, width), lambda b, i: (b, i, 0))
    wspec = lambda w: _resident((1,) + w.shape[1:], lambda b, i: (layer, 0, 0))
    return pl.pallas_call(
        _merge_kernel,
        out_shape=jax.ShapeDtypeStruct(x.shape, F32),
        grid=(nb, tb // tm),
        in_specs=[
            rspec(d), rspec(yc.shape[2]), rspec(mp.shape[2]), rspec(o.shape[2]),
            pl.BlockSpec((1, d), lambda b, i: (0, 0)),
            _row_spec(sh, tm), _row_spec(sc, tm), _row_spec(g2, tm),
            wspec(w_gate), wspec(w_conv_out), wspec(w_pool_out), wspec(w_attn_out), wspec(w_out),
        ],
        out_specs=rspec(d),
        compiler_params=_params(("parallel", "parallel")),
        name="merge",
    )(x, yc, mp, o, g, sh, sc, g2, w_gate, w_conv_out, w_pool_out, w_attn_out, w_out)


def _rope_tables(pos):
    half = HEAD_DIM // 2
    inv = ROPE_THETA ** (-jnp.arange(half, dtype=F32) / half)
    ang = pos.astype(F32)[:, None] * inv[None, :]
    cos, sin = jnp.cos(ang), jnp.sin(ang)
    return jnp.tile(cos, (1, 4)), jnp.concatenate([-sin, sin, -sin, sin], axis=-1)


def _tile_rows(total, want):
    t = min(want, total)
    assert total % t == 0, (total, t)
    return t


def kernel(x_prompt, x_sample, cache_attn_k, cache_attn_v, state_conv, state_pool, c_prompt, c_sample, w_ada, b_ada, g_ffn1, w_ffn1_in, w_ffn1_out, g_mix, w_in, w_dw, b_dw, ln_conv_g, ln_conv_b, w_conv_out, w_pool_grp, pool_scale, w_pool_out, g_q, g_k, lam_q1, lam_k1, lam_q2, lam_k2, g_sub, w_attn_out, w_out, g_ffn2, w_ffn2_in, w_ffn2_out):
    bp, tp, d = x_prompt.shape
    bs, ts, _ = x_sample.shape
    n_layers = w_ada.shape[0]
    past = cache_attn_k.shape[2]
    conv_w = w_dw.shape[2]
    pool_w = pool_scale.shape[1]
    attn_w = N_HEADS * V_DIM
    proj_w = 2 * conv_w + pool_w + 3 * attn_w
    assert CONV_HALO >= CONV_K - 1 and POOL_HALO >= POOL_BUF and tp >= CONV_HALO

    w_ffn1_in_b, w_ffn1_out_b = w_ffn1_in.astype(BF16), w_ffn1_out.astype(BF16)
    w_ffn2_in_b, w_ffn2_out_b = w_ffn2_in.astype(BF16), w_ffn2_out.astype(BF16)
    w_proj_b = w_in[:, :, :proj_w].astype(BF16)
    w_gate_b = w_in[:, :, proj_w:].astype(BF16)
    w_conv_out_b, w_pool_out_b = w_conv_out.astype(BF16), w_pool_out.astype(BF16)
    w_attn_out_b, w_out_b = w_attn_out.astype(BF16), w_out.astype(BF16)
    w_grp_b = w_pool_grp.astype(BF16)
    chunk_id = np.arange(attn_w) // HEAD_DIM
    bd = jnp.asarray(chunk_id[:, None] == chunk_id[None, :], BF16)

    rows = bp + bs
    rows_pad = -(-rows // 16) * 16
    c_all = jnp.concatenate([c_prompt, c_sample, jnp.zeros((rows_pad - rows, d), F32)], axis=0)
    mod = _ada(c_all, w_ada, b_ada).reshape(n_layers, rows_pad, N_MOD, d)

    cos_p, sin_p = _rope_tables(jnp.arange(tp))
    cos_s, sin_s = _rope_tables(past + jnp.arange(ts))
    cos_p, sin_p = cos_p[None], sin_p[None]
    cos_s, sin_s = jnp.tile(cos_s, (bs, 1))[None], jnp.tile(sin_s, (bs, 1))[None]

    ms = bs * ts
    cache_k = cache_attn_k.reshape(n_layers, bs, past, attn_w)
    cache_v = cache_attn_v.reshape(n_layers, bs, past, attn_w)
    tk_s = _tile_rows(past, 1024)

    tm_p = _tile_rows(tp, 256)
    tm_cp = _tile_rows(tp, 512)
    q_scale = HEAD_DIM ** -0.5 * math.log2(math.e)

    xp = x_prompt
    xs = x_sample.reshape(1, ms, d)
    outs = [[] for _ in range(8)]
    for l in range(n_layers):
        lam_init = 0.8 - 0.6 * math.exp(-0.3 * l)
        lam = (jnp.exp(jnp.sum(lam_q1[l] * lam_k1[l])) - jnp.exp(jnp.sum(lam_q2[l] * lam_k2[l])) + lam_init)
        lam = lam.reshape(1, 1).astype(F32)
        g_sub_eff = g_sub[l] * (1.0 - lam_init)
        gq_t = (jnp.tile(g_q[l], attn_w // HEAD_DIM) * q_scale)[None]
        gk_t = jnp.tile(g_k[l], attn_w // HEAD_DIM)[None]

        mod_p = [mod[l, :bp, k][:, None, :] for k in range(N_MOD)]
        mod_s = [jnp.repeat(mod[l, bp:rows, k], ts, axis=0)[None] for k in range(N_MOD)]

        def layer(x, m, tm, tm_c, cos, sin, conv_state, pool_state, pos0, attend):
            sh1, sc1, g1, sh2, sc2, g2, sh3, sc3, g3 = m
            x = _ffn(x, g_ffn1[l][None], sh1, sc1, g1, w_ffn1_in_b, w_ffn1_out_b, l, tm)
            z, p, k, v, qb, kb, vt = _inproj(x, g_mix[l][None], sh2, sc2, w_proj_b, bd, gq_t, gk_t,
                                             cos, sin, l, tm, conv_w, pool_w)
            nb = conv_state.shape[0]
            zb = z.reshape(nb, -1, conv_w)
            pb = p.reshape(nb, -1, pool_w)
            z_init = jnp.pad(conv_state, ((0, 0), (CONV_HALO - (CONV_K - 1), 0), (0, 0)))
            p_init = jnp.pad(pool_state, ((0, 0), (POOL_HALO - POOL_BUF, 0), (0, 0)))
            halo_ok = zb.shape[1] >= CONV_HALO
            yc, mp = _convpool(zb, zb if halo_ok else z_init, z_init, pb, pb if halo_ok else p_init, p_init,
                               w_dw[l], b_dw[l][None], ln_conv_g[l][None], ln_conv_b[l][None],
                               w_grp_b, pool_scale[l][None], l, tm_c, pos0)
            o = attend(qb, kb, v, vt)
            x = _merge(x, yc.reshape(x.shape[0], -1, conv_w), mp.reshape(x.shape[0], -1, pool_w), o,
                       g_mix[l][None], sh2, sc2, g2, w_gate_b, w_conv_out_b, w_pool_out_b, w_attn_out_b,
                       w_out_b, l, tm)
            x = _ffn(x, g_ffn2[l][None], sh3, sc3, g3, w_ffn2_in_b, w_ffn2_out_b, l, tm)
            new_conv = jnp.concatenate([conv_state, zb], axis=1)[:, -(CONV_K - 1):]
            new_pool = jnp.concatenate([pool_state, pb], axis=1)[:, -POOL_BUF:]
            return x, k, v, new_conv, new_pool

        def attend_p(qb, kb, v, vt):
            return _attn(qb, kb, vt, g_sub_eff.reshape(V_DIM, 1), lam)

        def attend_s(qb, kb, v, vt):
            o = _attn_s(qb.reshape(bs, ts, attn_w), kb.reshape(bs, ts, attn_w), v.reshape(bs, ts, attn_w),
                        cache_k, cache_v, g_sub_eff.reshape(1, V_DIM), lam, l, tk_s)
            return o.reshape(1, ms, attn_w)

        xp, kp, vp, cp, pp = layer(xp, mod_p, tm_p, tm_cp, cos_p, sin_p,
                                   jnp.zeros((bp, CONV_K - 1, conv_w), F32),
                                   jnp.zeros((bp, POOL_BUF, pool_w), F32), 0, attend_p)
        xs, kn, vn, cn, pn = layer(xs, mod_s, ms, ts, cos_s, sin_s, state_conv[l], state_pool[l], past, attend_s)
        for lst, val in zip(outs, (kp, vp, cp, pp, kn, vn, cn, pn)):
            lst.append(val)

    kp, vp, cp, pp, kn, vn, cn, pn = [jnp.stack(o) for o in outs]
    return (xp, xs.reshape(bs, ts, d),
            kp.reshape(n_layers, bp, tp, N_HEADS, 2, HEAD_DIM), vp.reshape(n_layers, bp, tp, N_HEADS, V_DIM),
            cp, pp,
            kn.reshape(n_layers, bs, ts, N_HEADS, 2, HEAD_DIM), vn.reshape(n_layers, bs, ts, N_HEADS, V_DIM),
            cn, pn)
```

```python
import functools
import math

import numpy as np
import jax
import jax.numpy as jnp
from jax import lax
from jax.experimental import pallas as pl
from jax.experimental.pallas import tpu as pltpu

F32 = jnp.float32
BF16 = jnp.bfloat16

EPS = 1e-6
CHUNK = 64
N_HEADS = 8
HEAD_DIM = 64
V_DIM = 2 * HEAD_DIM
CONV_K = 31
POOL_WINDOWS = (2, 4, 8, 16)
POOL_BUF = 15
ROPE_THETA = 10000.0
N_MOD = 9

SUBLANES = 8
LANES = 128
CONV_HALO = 32
POOL_HALO = 16
ATT_KB = 256
ATT_KS = 2 * ATT_KB
V_ROWS = V_DIM + 16
V7X_VMEM_LIMIT = 56 * 1024 * 1024


def _params(semantics):
    return pltpu.CompilerParams(dimension_semantics=semantics, vmem_limit_bytes=V7X_VMEM_LIMIT)


def _dot(a, b):
    return jnp.dot(a, b, preferred_element_type=F32)


def _dot_nt(a, b):
    return lax.dot_general(a, b, (((1,), (1,)), ((), ())), preferred_element_type=F32)


def _sigmoid(x):
    return 1.0 / (1.0 + jnp.exp(-x))


def _rms_mod(x, g, sh, sc):
    ms = jnp.mean(x * x, axis=-1, keepdims=True)
    y = x * lax.rsqrt(ms + EPS) * g
    return y * (1.0 + sc) + sh


def _row_spec(arr, tm):
    d = arr.shape[-1]
    if arr.shape[1] == 1:
        return pl.BlockSpec((1, 1, d), lambda b, i: (b, 0, 0))
    return pl.BlockSpec((1, tm, d), lambda b, i: (b, i, 0))


def _resident(shape, index_map):
    return pl.BlockSpec(shape, index_map, pipeline_mode=pl.Buffered(1))


def _ada_kernel(c_ref, w_ref, b_ref, o_ref):
    c = c_ref[...]
    a = (c * _sigmoid(c)).astype(BF16)
    o_ref[0] = _dot(a, w_ref[0].astype(BF16)) + b_ref[0]


def _ada(c_all, w_ada, b_ada):
    n_layers, d, n = w_ada.shape
    rows = c_all.shape[0]
    tn = 1024
    return pl.pallas_call(
        _ada_kernel,
        out_shape=jax.ShapeDtypeStruct((n_layers, rows, n), F32),
        grid=(n_layers, n // tn),
        in_specs=[
            pl.BlockSpec((rows, d), lambda l, j: (0, 0)),
            pl.BlockSpec((1, d, tn), lambda l, j: (l, 0, j)),
            pl.BlockSpec((1, 1, tn), lambda l, j: (l, 0, j)),
        ],
        out_specs=pl.BlockSpec((1, rows, tn), lambda l, j: (l, 0, j)),
        compiler_params=_params(("parallel", "parallel")),
        name="ada",
    )(c_all, w_ada, b_ada.reshape(n_layers, 1, n))


def _ffn_kernel(x_ref, g_ref, sh_ref, sc_ref, gt_ref, win_ref, wout_ref, o_ref, *, d_ff):
    x = x_ref[0]
    h = _rms_mod(x, g_ref[...], sh_ref[0], sc_ref[0]).astype(BF16)
    a = _dot(h, win_ref[0, :, :d_ff])
    b = _dot(h, win_ref[0, :, d_ff:])
    hid = (a * _sigmoid(a) * b).astype(BF16)
    y = _dot(hid, wout_ref[0])
    o_ref[0] = x + 0.5 * gt_ref[0] * y


def _ffn(x, g, sh, sc, gate, w_in, w_out, layer, tm):
    nb, tb, d = x.shape
    d_ff = w_out.shape[1]
    return pl.pallas_call(
        functools.partial(_ffn_kernel, d_ff=d_ff),
        out_shape=jax.ShapeDtypeStruct(x.shape, F32),
        grid=(nb, tb // tm),
        in_specs=[
            pl.BlockSpec((1, tm, d), lambda b, i: (b, i, 0)),
            pl.BlockSpec((1, d), lambda b, i: (0, 0)),
            _row_spec(sh, tm), _row_spec(sc, tm), _row_spec(gate, tm),
            _resident((1, d, 2 * d_ff), lambda b, i: (layer, 0, 0)),
            _resident((1, d_ff, d), lambda b, i: (layer, 0, 0)),
        ],
        out_specs=pl.BlockSpec((1, tm, d), lambda b, i: (b, i, 0)),
        compiler_params=_params(("parallel", "parallel")),
        name="ffn",
    )(x, g, sh, sc, gate, w_in, w_out)


def _norm_rope(u, bd_ref, g_t, cos, sin, f32_ref, bf16_ref):
    ssq = _dot((u * u).astype(BF16), bd_ref[...])
    y = u * lax.rsqrt(ssq * (1.0 / HEAD_DIM) + EPS) * g_t
    tm = u.shape[0]
    lane = lax.broadcasted_iota(jnp.int32, (tm, LANES), 1)
    first_half = (lane & (HEAD_DIM - 1)) < HEAD_DIM // 2
    for h in range(u.shape[1] // LANES):
        yh = y[:, h * LANES:(h + 1) * LANES]
        partner = jnp.where(first_half,
                            pltpu.roll(yh, LANES - HEAD_DIM // 2, 1),
                            pltpu.roll(yh, HEAD_DIM // 2, 1))
        r = yh * cos + partner * sin
        if f32_ref is not None:
            f32_ref[0, :, h * LANES:(h + 1) * LANES] = r
        bf16_ref[0, :, h * LANES:(h + 1) * LANES] = r.astype(BF16)


def _inproj_kernel(*refs, conv_w, pool_w, attn_w, n_alias, want_vt):
    (x_ref, g_ref, sh_ref, sc_ref, w_ref, bd_ref, gq_ref, gk_ref, cos_ref, sin_ref) = refs[:10]
    outs = refs[10 + n_alias:]
    z_ref, p_ref, k_stack_ref, v_stack_ref, qb_ref, kb_ref = outs[:6]
    k_ref, v_ref = k_stack_ref.at[0], v_stack_ref.at[0]
    x = x_ref[0]
    h = _rms_mod(x, g_ref[...], sh_ref[0], sc_ref[0]).astype(BF16)
    o = 0
    a = _dot(h, w_ref[0, :, o:o + conv_w]); o += conv_w
    gt = _dot(h, w_ref[0, :, o:o + conv_w]); o += conv_w
    z_ref[0] = a * _sigmoid(gt)
    p_ref[0] = _dot(h, w_ref[0, :, o:o + pool_w]); o += pool_w
    cos = cos_ref[0]
    sin = sin_ref[0]
    uq = _dot(h, w_ref[0, :, o:o + attn_w]); o += attn_w
    _norm_rope(uq, bd_ref, gq_ref[...], cos, sin, None, qb_ref)
    uk = _dot(h, w_ref[0, :, o:o + attn_w]); o += attn_w
    _norm_rope(uk, bd_ref, gk_ref[...], cos, sin, k_ref, kb_ref)
    uv = _dot(h, w_ref[0, :, o:o + attn_w])
    v_ref[0] = uv
    if want_vt:
        vt_ref = outs[6]
        tail = lax.broadcasted_iota(jnp.int32, (V_ROWS - V_DIM, uv.shape[0]), 0)
        ones_rows = jnp.where(tail == 0, 1.0, 0.0).astype(BF16)
        for hd in range(attn_w // V_DIM):
            vt_ref[0, hd, 0, 0:V_DIM, :] = uv[:, hd * V_DIM:(hd + 1) * V_DIM].T.astype(BF16)
            vt_ref[0, hd, 0, V_DIM:V_ROWS, :] = ones_rows


def _inproj(x, g, sh, sc, w, bd, gq_t, gk_t, cos, sin, layer, n_layers, tm, conv_w, pool_w, kv_stacks, want_vt):
    nb, tb, d = x.shape
    attn_w = N_HEADS * V_DIM
    n = w.shape[2]
    row = lambda width, dt: jax.ShapeDtypeStruct((nb, tb, width), dt)
    rspec = lambda width: pl.BlockSpec((1, tm, width), lambda b, i: (b, i, 0))
    stack = jax.ShapeDtypeStruct((n_layers, nb, tb, attn_w), F32)
    stack_spec = pl.BlockSpec((1, 1, tm, attn_w), lambda b, i: (layer, b, i, 0))
    out_shape = [row(conv_w, F32), row(pool_w, F32), stack, stack, row(attn_w, BF16), row(attn_w, BF16)]
    out_specs = [rspec(conv_w), rspec(pool_w), stack_spec, stack_spec, rspec(attn_w), rspec(attn_w)]
    if want_vt:
        per = ATT_KS // tm
        assert ATT_KS % tm == 0 and tb % ATT_KS == 0 and tm % LANES == 0
        out_shape.append(jax.ShapeDtypeStruct((nb, N_HEADS, tb // ATT_KS, V_ROWS, ATT_KS), BF16))
        out_specs.append(pl.BlockSpec((1, N_HEADS, 1, V_ROWS, tm), lambda b, i: (b, 0, i // per, 0, i % per)))
    in_specs = [
        rspec(d),
        pl.BlockSpec((1, d), lambda b, i: (0, 0)),
        _row_spec(sh, tm), _row_spec(sc, tm),
        _resident((1, d, n), lambda b, i: (layer, 0, 0)),
        _resident((attn_w, attn_w), lambda b, i: (0, 0)),
        pl.BlockSpec((1, attn_w), lambda b, i: (0, 0)),
        pl.BlockSpec((1, attn_w), lambda b, i: (0, 0)),
        pl.BlockSpec((1, tm, LANES), lambda b, i: (0, i, 0)),
        pl.BlockSpec((1, tm, LANES), lambda b, i: (0, i, 0)),
    ]
    args = [x, g, sh, sc, w, bd, gq_t, gk_t, cos, sin]
    aliases = {}
    if kv_stacks is not None:
        aliases = {len(args): 2, len(args) + 1: 3}
        in_specs += [pl.BlockSpec(memory_space=pl.ANY)] * 2
        args += list(kv_stacks)
    return pl.pallas_call(
        functools.partial(_inproj_kernel, conv_w=conv_w, pool_w=pool_w, attn_w=attn_w,
                          n_alias=len(aliases), want_vt=want_vt),
        out_shape=tuple(out_shape),
        grid=(nb, tb // tm),
        in_specs=in_specs,
        out_specs=tuple(out_specs),
        input_output_aliases=aliases,
        compiler_params=_params(("parallel", "parallel")),
        name="inproj",
    )(*args)


def _convpool_kernel(z_ref, zh_ref, zi_ref, p_ref, ph_ref, pi_ref, wdw_ref, bdw_ref, lng_ref, lnb_ref,
                     wg_ref, ps_ref, yc_ref, mp_ref, zc_s, zsh_s, pc_s, *, tm, pos0):
    i = pl.program_id(1)
    first = i == 0
    zc_s[0:CONV_HALO, :] = jnp.where(first, zi_ref[0], zh_ref[0])
    zc_s[CONV_HALO:CONV_HALO + tm, :] = z_ref[0]
    pc_s[0:POOL_HALO, :] = jnp.where(first, pi_ref[0], ph_ref[0])
    pc_s[POOL_HALO:POOL_HALO + tm, :] = p_ref[0]
    n_sh = zsh_s.shape[1]
    for k in range(1, SUBLANES):
        zsh_s[k] = zc_s[k:k + n_sh, :]

    width = zc_s.shape[1]
    rc = min(32, tm)
    for c in range(tm // rc):
        r0 = c * rc
        acc = jnp.broadcast_to(bdw_ref[...], (rc, width))
        for j in range(CONV_K):
            off = r0 + CONV_HALO - (CONV_K - 1) + j
            k, base = off % SUBLANES, off - off % SUBLANES
            tap = zc_s[base:base + rc, :] if k == 0 else zsh_s[k, base:base + rc, :]
            acc = acc + wdw_ref[j:j + 1, :] * tap
        mu = jnp.mean(acc, axis=-1, keepdims=True)
        dlt = acc - mu
        var = jnp.mean(dlt * dlt, axis=-1, keepdims=True)
        yn = dlt * lax.rsqrt(var + EPS) * lng_ref[...] + lnb_ref[...]
        yc_ref[0, r0:r0 + rc, :] = (yn * _sigmoid(yn)).astype(BF16)

    gw = width // len(POOL_WINDOWS)
    rp = min(128, tm)
    for c in range(tm // rp):
        r0 = c * rp
        pos = lax.broadcasted_iota(jnp.int32, (rp, 1), 0) + (pos0 + i * tm + r0)
        for g, w in enumerate(POOL_WINDOWS):
            lo = g * gw
            cur = pc_s[POOL_HALO + r0:POOL_HALO + r0 + rp, lo:lo + gw]
            acc = cur
            for s in range(1, w):
                acc = acc + pc_s[POOL_HALO + r0 - s:POOL_HALO + r0 - s + rp, lo:lo + gw]
            cnt = jnp.minimum(w, pos + 1).astype(F32)
            dlt = acc / cnt - cur
            m = _dot(dlt.astype(BF16), wg_ref[0, g]) * ps_ref[:, lo:lo + gw]
            mp_ref[0, r0:r0 + rp, lo:lo + gw] = m.astype(BF16)


def _convpool(z, z_halo_src, z_init, p, p_halo_src, p_init, w_dw, b_dw, ln_g, ln_b, w_grp, pool_scale,
              layer, tm, pos0):
    nb, tb, width = z.shape
    gw = width // len(POOL_WINDOWS)
    zsteps, psteps = tm // CONV_HALO, tm // POOL_HALO
    vec = pl.BlockSpec((1, width), lambda b, i: (0, 0))
    return pl.pallas_call(
        functools.partial(_convpool_kernel, tm=tm, pos0=pos0),
        out_shape=(jax.ShapeDtypeStruct((nb, tb, width), BF16), jax.ShapeDtypeStruct((nb, tb, width), BF16)),
        grid=(nb, tb // tm),
        in_specs=[
            pl.BlockSpec((1, tm, width), lambda b, i: (b, i, 0)),
            pl.BlockSpec((1, CONV_HALO, width), lambda b, i: (b, jnp.maximum(i * zsteps - 1, 0), 0)),
            pl.BlockSpec((1, CONV_HALO, width), lambda b, i: (b, 0, 0)),
            pl.BlockSpec((1, tm, width), lambda b, i: (b, i, 0)),
            pl.BlockSpec((1, POOL_HALO, width), lambda b, i: (b, jnp.maximum(i * psteps - 1, 0), 0)),
            pl.BlockSpec((1, POOL_HALO, width), lambda b, i: (b, 0, 0)),
            pl.BlockSpec((CONV_K, width), lambda b, i: (0, 0)),
            vec, vec, vec,
            pl.BlockSpec((1, len(POOL_WINDOWS), gw, gw), lambda b, i: (layer, 0, 0, 0)),
            vec,
        ],
        out_specs=(pl.BlockSpec((1, tm, width), lambda b, i: (b, i, 0)),
                   pl.BlockSpec((1, tm, width), lambda b, i: (b, i, 0))),
        scratch_shapes=[pltpu.VMEM((CONV_HALO + tm, width), F32),
                        pltpu.VMEM((SUBLANES, CONV_HALO + tm - SUBLANES, width), F32),
                        pltpu.VMEM((POOL_HALO + tm, width), F32)],
        compiler_params=_params(("parallel", "arbitrary")),
        name="convpool",
    )(z, z_halo_src, z_init, p, p_halo_src, p_init, w_dw, b_dw, ln_g, ln_b, w_grp, pool_scale)


def _attn_kernel(q_ref, k_ref, vt_ref, gs_ref, lam_ref, o_ref, qbd_s, s_s, m_s, acc_s, *, tq):
    kb = ATT_KB
    ks = ATT_KS
    gwid = 2 * kb
    ng = tq // kb
    assert tq == 2 * ks and ks == 2 * kb
    qi = pl.program_id(2)

    qt = q_ref[0].astype(F32).T
    row = lax.broadcasted_iota(jnp.int32, (V_DIM, kb), 0)
    for c in range(ng):
        blk = qt[:, c * kb:(c + 1) * kb]
        qbd_s[:, c * gwid:c * gwid + kb] = jnp.where(row < HEAD_DIM, blk, 0.0).astype(BF16)
        qbd_s[:, c * gwid + kb:(c + 1) * gwid] = jnp.where(row >= HEAD_DIM, blk, 0.0).astype(BF16)
    m_s[...] = jnp.full(m_s.shape, -jnp.inf, F32)
    acc_s[...] = jnp.zeros(acc_s.shape, F32)

    def scores(js, lo):
        keys = k_ref[0, pl.ds(pl.multiple_of(js * ks, ks), ks), :]
        return _dot(keys, qbd_s[:, lo:])

    def update(st, vt, lo):
        m_prev = m_s[:, lo:]
        m_new = jnp.maximum(m_prev, jnp.max(st, axis=0, keepdims=True))
        alpha = jnp.exp2(m_prev - m_new)
        pt = jnp.exp2(st - m_new).astype(BF16)
        acc_s[:, lo:] = acc_s[:, lo:] * alpha + _dot(vt, pt)
        m_s[:, lo:] = m_new

    n_full = 2 * qi
    s_s[0] = scores(0, 0)

    def pair(jj, carry):
        j0 = 2 * jj
        s_s[1] = scores(j0 + 1, 0)
        update(s_s[0], vt_ref[0, 0, j0], 0)
        s_s[0] = scores(j0 + 2, 0)
        update(s_s[1], vt_ref[0, 0, j0 + 1], 0)
        return carry

    lax.fori_loop(0, qi, pair, 0)

    krow = lax.broadcasted_iota(jnp.int32, (kb, kb), 0)
    qcol = lax.broadcasted_iota(jnp.int32, (kb, kb), 1)
    shift = CHUNK.bit_length() - 1
    vis = lax.shift_right_logical(krow, shift) <= lax.shift_right_logical(qcol, shift)

    def masked(st):
        parts = [jnp.where(vis, st[:, :kb], -jnp.inf), jnp.where(vis, st[:, kb:gwid], -jnp.inf)]
        if st.shape[1] > gwid:
            parts.append(st[:, gwid:])
        return jnp.concatenate(parts, axis=1)

    s_s[1, :, 2 * gwid:] = scores(n_full + 1, 2 * gwid)
    for s in range(ng):
        buf, r0, lo = s // 2, (s % 2) * kb, s * gwid
        update(masked(s_s[buf, r0:r0 + kb, lo:]), vt_ref[0, 0, n_full + buf, :, r0:r0 + kb], lo)

    lam = lam_ref[0, 0]
    for c in range(ng):
        lo = c * gwid
        o1 = acc_s[0:V_DIM, lo:lo + kb] / acc_s[V_DIM:V_DIM + 1, lo:lo + kb]
        o2 = acc_s[0:V_DIM, lo + kb:lo + gwid] / acc_s[V_DIM:V_DIM + 1, lo + kb:lo + gwid]
        ot = o1 - lam * o2
        ms = jnp.mean(ot * ot, axis=0, keepdims=True)
        ot = ot * lax.rsqrt(ms + EPS) * gs_ref[...]
        o_ref[0, c * kb:(c + 1) * kb, :] = ot.T.astype(BF16)


def _attn(qb, kb, vt, g_sub_col, lam):
    nb, t, width = qb.shape
    tq = 2 * ATT_KS
    assert t % tq == 0
    return pl.pallas_call(
        functools.partial(_attn_kernel, tq=tq),
        out_shape=jax.ShapeDtypeStruct((nb, t, width), BF16),
        grid=(nb, N_HEADS, t // tq),
        in_specs=[
            pl.BlockSpec((1, tq, V_DIM), lambda b, h, i: (b, i, h)),
            pl.BlockSpec((1, t, V_DIM), lambda b, h, i: (b, 0, h)),
            pl.BlockSpec((1, 1, t // ATT_KS, V_ROWS, ATT_KS), lambda b, h, i: (b, h, 0, 0, 0)),
            pl.BlockSpec((V_DIM, 1), lambda b, h, i: (0, 0)),
            pl.BlockSpec(memory_space=pltpu.SMEM),
        ],
        out_specs=pl.BlockSpec((1, tq, V_DIM), lambda b, h, i: (b, i, h)),
        scratch_shapes=[
            pltpu.VMEM((V_DIM, 2 * tq), BF16),
            pltpu.VMEM((2, ATT_KS, 2 * tq), F32),
            pltpu.VMEM((1, 2 * tq), F32),
            pltpu.VMEM((V_ROWS, 2 * tq), F32),
        ],
        compiler_params=_params(("parallel", "parallel", "arbitrary")),
        name="attn",
    )(qb, kb, vt, g_sub_col, lam)


def _attn_s_kernel(q_ref, kn_ref, vn_ref, kc_ref, vc_ref, gs_ref, lam_ref, o_ref, qbd_s, m_s, l_s, acc_s,
                   *, ts, n_kv):
    j = pl.program_id(1)
    width = q_ref.shape[2]

    def fold(s, v_bf16, m_prev, l_prev, acc_prev):
        m_new = jnp.maximum(m_prev, jnp.max(s, axis=1, keepdims=True))
        alpha = jnp.exp2(m_prev - m_new)
        p = jnp.exp2(s - m_new)
        l_s[...] = alpha * l_prev + jnp.sum(p, axis=1, keepdims=True)
        acc_s[...] = alpha * acc_prev + _dot(p.astype(BF16), v_bf16)
        m_s[...] = m_new

    @pl.when(j == 0)
    def _init():
        q = q_ref[0].astype(F32)
        col = lax.broadcasted_iota(jnp.int32, (ts, width), 1)
        for r in range(N_HEADS * 2):
            mine = (col >= r * HEAD_DIM) & (col < (r + 1) * HEAD_DIM)
            qbd_s[r * ts:(r + 1) * ts, :] = jnp.where(mine, q, 0.0).astype(BF16)
        s = _dot_nt(qbd_s[...], kn_ref[0])
        rows = s.shape[0]
        s = jnp.where(lax.broadcasted_iota(jnp.int32, s.shape, 1) < ts, s, -jnp.inf)
        fold(s, vn_ref[0], jnp.full((rows, 1), -jnp.inf, F32), jnp.zeros((rows, 1), F32),
             jnp.zeros((rows, width), F32))

    s = _dot_nt(qbd_s[...], kc_ref[0, 0])
    fold(s, vc_ref[0, 0], m_s[...], l_s[...], acc_s[...])

    @pl.when(j == n_kv - 1)
    def _finish():
        lam = lam_ref[0, 0]
        for h in range(N_HEADS):
            r0 = h * 2 * ts
            c0 = h * V_DIM
            o1 = acc_s[r0:r0 + ts, c0:c0 + V_DIM] / l_s[r0:r0 + ts, :]
            o2 = acc_s[r0 + ts:r0 + 2 * ts, c0:c0 + V_DIM] / l_s[r0 + ts:r0 + 2 * ts, :]
            o = o1 - lam * o2
            ms = jnp.mean(o * o, axis=-1, keepdims=True)
            o_ref[0, :, c0:c0 + V_DIM] = (o * lax.rsqrt(ms + EPS) * gs_ref[...]).astype(BF16)


def _attn_s(qb, kb_new, v_new, cache_k, cache_v, g_sub_row, lam, layer, tk):
    nb, ts, width = qb.shape
    past = cache_k.shape[2]
    n_kv = past // tk
    rows = N_HEADS * 2 * ts
    pad = ((0, 0), (0, LANES - ts), (0, 0))
    kb_new = jnp.pad(kb_new, pad)
    v_new = jnp.pad(v_new.astype(BF16), pad)
    new = pl.BlockSpec((1, ts, width), lambda b, j: (b, 0, 0))
    new_kv = pl.BlockSpec((1, LANES, width), lambda b, j: (b, 0, 0))
    old = pl.BlockSpec((1, 1, tk, width), lambda b, j: (layer, b, j, 0))
    return pl.pallas_call(
        functools.partial(_attn_s_kernel, ts=ts, n_kv=n_kv),
        out_shape=jax.ShapeDtypeStruct((nb, ts, width), BF16),
        grid=(nb, n_kv),
        in_specs=[new, new_kv, new_kv, old, old,
                  pl.BlockSpec((1, V_DIM), lambda b, j: (0, 0)),
                  pl.BlockSpec(memory_space=pltpu.SMEM)],
        out_specs=new,
        scratch_shapes=[
            pltpu.VMEM((rows, width), BF16),
            pltpu.VMEM((rows, 1), F32),
            pltpu.VMEM((rows, 1), F32),
            pltpu.VMEM((rows, width), F32),
        ],
        compiler_params=_params(("parallel", "arbitrary")),
        name="attn_s",
    )(qb, kb_new, v_new, cache_k, cache_v, g_sub_row, lam)


def _merge_kernel(x_ref, yc_ref, mp_ref, o_ref, g_ref, sh_ref, sc_ref, g2_ref,
                  wg_ref, wc_ref, wp_ref, wa_ref, wo_ref, out_ref):
    x = x_ref[0]
    d = x.shape[1]
    h = _rms_mod(x, g_ref[...], sh_ref[0], sc_ref[0]).astype(BF16)
    merged = _sigmoid(_dot(h, wg_ref[0, :, 0:d])) * _dot(yc_ref[0], wc_ref[0])
    merged = merged + _sigmoid(_dot(h, wg_ref[0, :, d:2 * d])) * _dot(mp_ref[0], wp_ref[0])
    merged = merged + _sigmoid(_dot(h, wg_ref[0, :, 2 * d:3 * d])) * _dot(o_ref[0], wa_ref[0])
    out_ref[0] = x + g2_ref[0] * _dot(merged.astype(BF16), wo_ref[0])


def _merge(x, yc, mp, o, g, sh, sc, g2, w_gate, w_conv_out, w_pool_out, w_attn_out, w_out, layer, tm):
    nb, tb, d = x.shape
    rspec = lambda width: pl.BlockSpec((1, tm, width), lambda b, i: (b, i, 0))
    wspec = lambda w: _resident((1,) + w.shape[1:], lambda b, i: (layer, 0, 0))
    return pl.pallas_call(
        _merge_kernel,
        out_shape=jax.ShapeDtypeStruct(x.shape, F32),
        grid=(nb, tb // tm),
        in_specs=[
            rspec(d), rspec(yc.shape[2]), rspec(mp.shape[2]), rspec(o.shape[2]),
            pl.BlockSpec((1, d), lambda b, i: (0, 0)),
            _row_spec(sh, tm), _row_spec(sc, tm), _row_spec(g2, tm),
            wspec(w_gate), wspec(w_conv_out), wspec(w_pool_out), wspec(w_attn_out), wspec(w_out),
        ],
        out_specs=rspec(d),
        compiler_params=_params(("parallel", "parallel")),
        name="merge",
    )(x, yc, mp, o, g, sh, sc, g2, w_gate, w_conv_out, w_pool_out, w_attn_out, w_out)


def _rope_tables(pos):
    half = HEAD_DIM // 2
    inv = ROPE_THETA ** (-jnp.arange(half, dtype=F32) / half)
    ang = pos.astype(F32)[:, None] * inv[None, :]
    cos, sin = jnp.cos(ang), jnp.sin(ang)
    return jnp.tile(cos, (1, 4)), jnp.concatenate([-sin, sin, -sin, sin], axis=-1)


def _tile_rows(total, want):
    t = min(want, total)
    assert total % t == 0, (total, t)
    return t


def kernel(x_prompt, x_sample, cache_attn_k, cache_attn_v, state_conv, state_pool, c_prompt, c_sample, w_ada, b_ada, g_ffn1, w_ffn1_in, w_ffn1_out, g_mix, w_in, w_dw, b_dw, ln_conv_g, ln_conv_b, w_conv_out, w_pool_grp, pool_scale, w_pool_out, g_q, g_k, lam_q1, lam_k1, lam_q2, lam_k2, g_sub, w_attn_out, w_out, g_ffn2, w_ffn2_in, w_ffn2_out):
    bp, tp, d = x_prompt.shape
    bs, ts, _ = x_sample.shape
    n_layers = w_ada.shape[0]
    past = cache_attn_k.shape[2]
    conv_w = w_dw.shape[2]
    pool_w = pool_scale.shape[1]
    attn_w = N_HEADS * V_DIM
    proj_w = 2 * conv_w + pool_w + 3 * attn_w
    assert CONV_HALO >= CONV_K - 1 and POOL_HALO >= POOL_BUF and tp >= CONV_HALO

    w_ffn1_in_b, w_ffn1_out_b = w_ffn1_in.astype(BF16), w_ffn1_out.astype(BF16)
    w_ffn2_in_b, w_ffn2_out_b = w_ffn2_in.astype(BF16), w_ffn2_out.astype(BF16)
    w_proj_b = w_in[:, :, :proj_w].astype(BF16)
    w_gate_b = w_in[:, :, proj_w:].astype(BF16)
    w_conv_out_b, w_pool_out_b = w_conv_out.astype(BF16), w_pool_out.astype(BF16)
    w_attn_out_b, w_out_b = w_attn_out.astype(BF16), w_out.astype(BF16)
    w_grp_b = w_pool_grp.astype(BF16)
    chunk_id = np.arange(attn_w) // HEAD_DIM
    bd = jnp.asarray(chunk_id[:, None] == chunk_id[None, :], BF16)

    rows = bp + bs
    rows_pad = -(-rows // 16) * 16
    c_all = jnp.concatenate([c_prompt, c_sample, jnp.zeros((rows_pad - rows, d), F32)], axis=0)
    mod = _ada(c_all, w_ada, b_ada).reshape(n_layers, rows_pad, N_MOD, d)

    cos_p, sin_p = _rope_tables(jnp.arange(tp))
    cos_s, sin_s = _rope_tables(past + jnp.arange(ts))
    cos_p, sin_p = cos_p[None], sin_p[None]
    cos_s, sin_s = jnp.tile(cos_s, (bs, 1))[None], jnp.tile(sin_s, (bs, 1))[None]

    ms = bs * ts
    cache_k = cache_attn_k.reshape(n_layers, bs, past, attn_w).astype(BF16)
    cache_v = cache_attn_v.reshape(n_layers, bs, past, attn_w).astype(BF16)
    tk_s = _tile_rows(past, 1024)

    tm_p = _tile_rows(tp, 256)
    tm_cp = _tile_rows(tp, 512)
    q_scale = HEAD_DIM ** -0.5 * math.log2(math.e)

    xp = x_prompt
    xs = x_sample.reshape(1, ms, d)
    kv_p = kv_s = None
    outs = [[] for _ in range(4)]
    for l in range(n_layers):
        lam_init = 0.8 - 0.6 * math.exp(-0.3 * l)
        lam = (jnp.exp(jnp.sum(lam_q1[l] * lam_k1[l])) - jnp.exp(jnp.sum(lam_q2[l] * lam_k2[l])) + lam_init)
        lam = lam.reshape(1, 1).astype(F32)
        g_sub_eff = g_sub[l] * (1.0 - lam_init)
        gq_t = (jnp.tile(g_q[l], attn_w // HEAD_DIM) * q_scale)[None]
        gk_t = jnp.tile(g_k[l], attn_w // HEAD_DIM)[None]

        mod_p = [mod[l, :bp, k][:, None, :] for k in range(N_MOD)]
        mod_s = [jnp.repeat(mod[l, bp:rows, k], ts, axis=0)[None] for k in range(N_MOD)]

        def layer(x, m, tm, tm_c, cos, sin, conv_state, pool_state, pos0, kv_stacks, attend, want_vt):
            sh1, sc1, g1, sh2, sc2, g2, sh3, sc3, g3 = m
            tm_d = min(2 * tm, x.shape[1])
            x = _ffn(x, g_ffn1[l][None], sh1, sc1, g1, w_ffn1_in_b, w_ffn1_out_b, l, tm_d)
            z, p, k_stack, v_stack, qb, kb, *vt = _inproj(
                x, g_mix[l][None], sh2, sc2, w_proj_b, bd, gq_t, gk_t, cos, sin, l, n_layers, tm,
                conv_w, pool_w, kv_stacks, want_vt)
            nb = conv_state.shape[0]
            zb = z.reshape(nb, -1, conv_w)
            pb = p.reshape(nb, -1, pool_w)
            z_init = jnp.pad(conv_state, ((0, 0), (CONV_HALO - (CONV_K - 1), 0), (0, 0)))
            p_init = jnp.pad(pool_state, ((0, 0), (POOL_HALO - POOL_BUF, 0), (0, 0)))
            halo_ok = zb.shape[1] >= CONV_HALO
            yc, mp = _convpool(zb, zb if halo_ok else z_init, z_init, pb, pb if halo_ok else p_init, p_init,
                               w_dw[l], b_dw[l][None], ln_conv_g[l][None], ln_conv_b[l][None],
                               w_grp_b, pool_scale[l][None], l, tm_c, pos0)
            o = attend(qb, kb, v_stack, *vt)
            x = _merge(x, yc.reshape(x.shape[0], -1, conv_w), mp.reshape(x.shape[0], -1, pool_w), o,
                       g_mix[l][None], sh2, sc2, g2, w_gate_b, w_conv_out_b, w_pool_out_b, w_attn_out_b,
                       w_out_b, l, tm_d)
            x = _ffn(x, g_ffn2[l][None], sh3, sc3, g3, w_ffn2_in_b, w_ffn2_out_b, l, tm_d)
            new_conv = jnp.concatenate([conv_state, zb], axis=1)[:, -(CONV_K - 1):]
            new_pool = jnp.concatenate([pool_state, pb], axis=1)[:, -POOL_BUF:]
            return x, (k_stack, v_stack), new_conv, new_pool

        def attend_p(qb, kb, v_stack, vt):
            return _attn(qb, kb, vt, g_sub_eff.reshape(V_DIM, 1), lam)

        def attend_s(qb, kb, v_stack):
            o = _attn_s(qb.reshape(bs, ts, attn_w), kb.reshape(bs, ts, attn_w), v_stack[l].reshape(bs, ts, attn_w),
                        cache_k, cache_v, g_sub_eff.reshape(1, V_DIM), lam, l, tk_s)
            return o.reshape(1, ms, attn_w)

        xp, kv_p, cp, pp = layer(xp, mod_p, tm_p, tm_cp, cos_p, sin_p,
                                 jnp.zeros((bp, CONV_K - 1, conv_w), F32),
                                 jnp.zeros((bp, POOL_BUF, pool_w), F32), 0, kv_p, attend_p, True)
        xs, kv_s, cn, pn = layer(xs, mod_s, ms, ts, cos_s, sin_s, state_conv[l], state_pool[l], past,
                                 kv_s, attend_s, False)
        for lst, val in zip(outs, (cp, pp, cn, pn)):
            lst.append(val)

    cp, pp, cn, pn = [jnp.stack(o) for o in outs]
    return (xp, xs.reshape(bs, ts, d),
            kv_p[0].reshape(n_layers, bp, tp, N_HEADS, 2, HEAD_DIM), kv_p[1].reshape(n_layers, bp, tp, N_HEADS, V_DIM),
            cp, pp,
            kv_s[0].reshape(n_layers, bs, ts, N_HEADS, 2, HEAD_DIM), kv_s[1].reshape(n_layers, bs, ts, N_HEADS, V_DIM),
            cn, pn)
```

```python
import functools
import math

import numpy as np
import jax
import jax.numpy as jnp
from jax import lax
from jax.experimental import pallas as pl
from jax.experimental.pallas import tpu as pltpu

F32 = jnp.float32
BF16 = jnp.bfloat16

EPS = 1e-6
CHUNK = 64
N_HEADS = 8
HEAD_DIM = 64
V_DIM = 2 * HEAD_DIM
CONV_K = 31
POOL_WINDOWS = (2, 4, 8, 16)
POOL_BUF = 15
ROPE_THETA = 10000.0
N_MOD = 9

SUBLANES = 8
LANES = 128
CONV_HALO = 32
POOL_HALO = 16
ATT_KB = 256
ATT_KS = 2 * ATT_KB
V_ROWS = V_DIM + 16
V7X_VMEM_LIMIT = 56 * 1024 * 1024


def _params(semantics):
    return pltpu.CompilerParams(dimension_semantics=semantics, vmem_limit_bytes=V7X_VMEM_LIMIT)


def _dot(a, b):
    return jnp.dot(a, b, preferred_element_type=F32)


def _dot_nt(a, b):
    return lax.dot_general(a, b, (((1,), (1,)), ((), ())), preferred_element_type=F32)


def _sigmoid(x):
    return 1.0 / (1.0 + jnp.exp(-x))


def _rms_mod(x, g, sh, sc):
    ms = jnp.mean(x * x, axis=-1, keepdims=True)
    y = x * lax.rsqrt(ms + EPS) * g
    return y * (1.0 + sc) + sh


def _row_spec(arr, tm):
    d = arr.shape[-1]
    if arr.shape[1] == 1:
        return pl.BlockSpec((1, 1, d), lambda b, i: (b, 0, 0))
    return pl.BlockSpec((1, tm, d), lambda b, i: (b, i, 0))


def _resident(shape, index_map):
    return pl.BlockSpec(shape, index_map, pipeline_mode=pl.Buffered(1))


def _ada_kernel(c_ref, w_ref, b_ref, o_ref):
    c = c_ref[...]
    a = (c * _sigmoid(c)).astype(BF16)
    o_ref[0] = _dot(a, w_ref[0].astype(BF16)) + b_ref[0]


def _ada(c_all, w_ada, b_ada):
    n_layers, d, n = w_ada.shape
    rows = c_all.shape[0]
    tn = 1024
    return pl.pallas_call(
        _ada_kernel,
        out_shape=jax.ShapeDtypeStruct((n_layers, rows, n), F32),
        grid=(n_layers, n // tn),
        in_specs=[
            pl.BlockSpec((rows, d), lambda l, j: (0, 0)),
            pl.BlockSpec((1, d, tn), lambda l, j: (l, 0, j)),
            pl.BlockSpec((1, 1, tn), lambda l, j: (l, 0, j)),
        ],
        out_specs=pl.BlockSpec((1, rows, tn), lambda l, j: (l, 0, j)),
        compiler_params=_params(("parallel", "parallel")),
        name="ada",
    )(c_all, w_ada, b_ada.reshape(n_layers, 1, n))


def _ffn_kernel(x_ref, g_ref, sh_ref, sc_ref, gt_ref, win_ref, wout_ref, o_ref, *, d_ff):
    x = x_ref[0]
    h = _rms_mod(x, g_ref[...], sh_ref[0], sc_ref[0]).astype(BF16)
    a = _dot(h, win_ref[0, :, :d_ff])
    b = _dot(h, win_ref[0, :, d_ff:])
    hid = (a * _sigmoid(a) * b).astype(BF16)
    y = _dot(hid, wout_ref[0])
    o_ref[0] = x + 0.5 * gt_ref[0] * y


def _ffn(x, g, sh, sc, gate, w_in, w_out, layer, tm):
    nb, tb, d = x.shape
    d_ff = w_out.shape[1]
    return pl.pallas_call(
        functools.partial(_ffn_kernel, d_ff=d_ff),
        out_shape=jax.ShapeDtypeStruct(x.shape, F32),
        grid=(nb, tb // tm),
        in_specs=[
            pl.BlockSpec((1, tm, d), lambda b, i: (b, i, 0)),
            pl.BlockSpec((1, d), lambda b, i: (0, 0)),
            _row_spec(sh, tm), _row_spec(sc, tm), _row_spec(gate, tm),
            _resident((1, d, 2 * d_ff), lambda b, i: (layer, 0, 0)),
            _resident((1, d_ff, d), lambda b, i: (layer, 0, 0)),
        ],
        out_specs=pl.BlockSpec((1, tm, d), lambda b, i: (b, i, 0)),
        compiler_params=_params(("parallel", "parallel")),
        name="ffn",
    )(x, g, sh, sc, gate, w_in, w_out)


def _norm_rope(u, bd_ref, g_t, cos, sin, f32_ref, bf16_ref):
    ssq = _dot((u * u).astype(BF16), bd_ref[...])
    y = u * lax.rsqrt(ssq * (1.0 / HEAD_DIM) + EPS) * g_t
    tm = u.shape[0]
    lane = lax.broadcasted_iota(jnp.int32, (tm, LANES), 1)
    first_half = (lane & (HEAD_DIM - 1)) < HEAD_DIM // 2
    for h in range(u.shape[1] // LANES):
        yh = y[:, h * LANES:(h + 1) * LANES]
        partner = jnp.where(first_half,
                            pltpu.roll(yh, LANES - HEAD_DIM // 2, 1),
                            pltpu.roll(yh, HEAD_DIM // 2, 1))
        r = yh * cos + partner * sin
        if f32_ref is not None:
            f32_ref[0, :, h * LANES:(h + 1) * LANES] = r
        bf16_ref[0, :, h * LANES:(h + 1) * LANES] = r.astype(BF16)


def _inproj_kernel(*refs, conv_w, pool_w, attn_w, n_alias, want_vt):
    (x_ref, g_ref, sh_ref, sc_ref, w_ref, bd_ref, gq_ref, gk_ref, cos_ref, sin_ref) = refs[:10]
    outs = refs[10 + n_alias:]
    z_ref, p_ref, k_stack_ref, v_stack_ref, qb_ref, kb_ref = outs[:6]
    k_ref, v_ref = k_stack_ref.at[0], v_stack_ref.at[0]
    x = x_ref[0]
    h = _rms_mod(x, g_ref[...], sh_ref[0], sc_ref[0]).astype(BF16)
    o = 0
    a = _dot(h, w_ref[0, :, o:o + conv_w]); o += conv_w
    gt = _dot(h, w_ref[0, :, o:o + conv_w]); o += conv_w
    z_ref[0] = a * _sigmoid(gt)
    p_ref[0] = _dot(h, w_ref[0, :, o:o + pool_w]); o += pool_w
    cos = cos_ref[0]
    sin = sin_ref[0]
    uq = _dot(h, w_ref[0, :, o:o + attn_w]); o += attn_w
    _norm_rope(uq, bd_ref, gq_ref[...], cos, sin, None, qb_ref)
    uk = _dot(h, w_ref[0, :, o:o + attn_w]); o += attn_w
    _norm_rope(uk, bd_ref, gk_ref[...], cos, sin, k_ref, kb_ref)
    uv = _dot(h, w_ref[0, :, o:o + attn_w])
    v_ref[0] = uv
    if want_vt:
        vt_ref = outs[6]
        tail = lax.broadcasted_iota(jnp.int32, (V_ROWS - V_DIM, uv.shape[0]), 0)
        ones_rows = jnp.where(tail == 0, 1.0, 0.0).astype(BF16)
        for hd in range(attn_w // V_DIM):
            vt_ref[0, hd, 0, 0:V_DIM, :] = uv[:, hd * V_DIM:(hd + 1) * V_DIM].T.astype(BF16)
            vt_ref[0, hd, 0, V_DIM:V_ROWS, :] = ones_rows


def _inproj(x, g, sh, sc, w, bd, gq_t, gk_t, cos, sin, layer, n_layers, tm, conv_w, pool_w, kv_stacks, want_vt):
    nb, tb, d = x.shape
    attn_w = N_HEADS * V_DIM
    n = w.shape[2]
    row = lambda width, dt: jax.ShapeDtypeStruct((nb, tb, width), dt)
    rspec = lambda width: pl.BlockSpec((1, tm, width), lambda b, i: (b, i, 0))
    stack = jax.ShapeDtypeStruct((n_layers, nb, tb, attn_w), F32)
    stack_spec = pl.BlockSpec((1, 1, tm, attn_w), lambda b, i: (layer, b, i, 0))
    out_shape = [row(conv_w, F32), row(pool_w, F32), stack, stack, row(attn_w, BF16), row(attn_w, BF16)]
    out_specs = [rspec(conv_w), rspec(pool_w), stack_spec, stack_spec, rspec(attn_w), rspec(attn_w)]
    if want_vt:
        per = ATT_KS // tm
        assert ATT_KS % tm == 0 and tb % ATT_KS == 0 and tm % LANES == 0
        out_shape.append(jax.ShapeDtypeStruct((nb, N_HEADS, tb // ATT_KS, V_ROWS, ATT_KS), BF16))
        out_specs.append(pl.BlockSpec((1, N_HEADS, 1, V_ROWS, tm), lambda b, i: (b, 0, i // per, 0, i % per)))
    in_specs = [
        rspec(d),
        pl.BlockSpec((1, d), lambda b, i: (0, 0)),
        _row_spec(sh, tm), _row_spec(sc, tm),
        _resident((1, d, n), lambda b, i: (layer, 0, 0)),
        _resident((attn_w, attn_w), lambda b, i: (0, 0)),
        pl.BlockSpec((1, attn_w), lambda b, i: (0, 0)),
        pl.BlockSpec((1, attn_w), lambda b, i: (0, 0)),
        pl.BlockSpec((1, tm, LANES), lambda b, i: (0, i, 0)),
        pl.BlockSpec((1, tm, LANES), lambda b, i: (0, i, 0)),
    ]
    args = [x, g, sh, sc, w, bd, gq_t, gk_t, cos, sin]
    aliases = {}
    if kv_stacks is not None:
        aliases = {len(args): 2, len(args) + 1: 3}
        in_specs += [pl.BlockSpec(memory_space=pl.ANY)] * 2
        args += list(kv_stacks)
    return pl.pallas_call(
        functools.partial(_inproj_kernel, conv_w=conv_w, pool_w=pool_w, attn_w=attn_w,
                          n_alias=len(aliases), want_vt=want_vt),
        out_shape=tuple(out_shape),
        grid=(nb, tb // tm),
        in_specs=in_specs,
        out_specs=tuple(out_specs),
        input_output_aliases=aliases,
        compiler_params=_params(("parallel", "parallel")),
        name="inproj",
    )(*args)


def _convpool_kernel(z_ref, zh_ref, zi_ref, p_ref, ph_ref, pi_ref, wdw_ref, bdw_ref, lng_ref, lnb_ref,
                     wg_ref, ps_ref, yc_ref, mp_ref, zc_s, zsh_s, pc_s, *, tm, pos0):
    i = pl.program_id(1)
    first = i == 0
    zc_s[0:CONV_HALO, :] = jnp.where(first, zi_ref[0], zh_ref[0])
    zc_s[CONV_HALO:CONV_HALO + tm, :] = z_ref[0]
    pc_s[0:POOL_HALO, :] = jnp.where(first, pi_ref[0], ph_ref[0])
    pc_s[POOL_HALO:POOL_HALO + tm, :] = p_ref[0]
    n_sh = zsh_s.shape[1]
    for k in range(1, SUBLANES):
        zsh_s[k] = zc_s[k:k + n_sh, :]

    width = zc_s.shape[1]
    rc = min(32, tm)
    for c in range(tm // rc):
        r0 = c * rc
        acc = jnp.broadcast_to(bdw_ref[...], (rc, width))
        for j in range(CONV_K):
            off = r0 + CONV_HALO - (CONV_K - 1) + j
            k, base = off % SUBLANES, off - off % SUBLANES
            tap = zc_s[base:base + rc, :] if k == 0 else zsh_s[k, base:base + rc, :]
            acc = acc + wdw_ref[j:j + 1, :] * tap
        mu = jnp.mean(acc, axis=-1, keepdims=True)
        dlt = acc - mu
        var = jnp.mean(dlt * dlt, axis=-1, keepdims=True)
        yn = dlt * lax.rsqrt(var + EPS) * lng_ref[...] + lnb_ref[...]
        yc_ref[0, r0:r0 + rc, :] = (yn * _sigmoid(yn)).astype(BF16)

    gw = width // len(POOL_WINDOWS)
    rp = min(128, tm)
    for c in range(tm // rp):
        r0 = c * rp
        pos = lax.broadcasted_iota(jnp.int32, (rp, 1), 0) + (pos0 + i * tm + r0)
        for g, w in enumerate(POOL_WINDOWS):
            lo = g * gw
            cur = pc_s[POOL_HALO + r0:POOL_HALO + r0 + rp, lo:lo + gw]
            acc = cur
            for s in range(1, w):
                acc = acc + pc_s[POOL_HALO + r0 - s:POOL_HALO + r0 - s + rp, lo:lo + gw]
            cnt = jnp.minimum(w, pos + 1).astype(F32)
            dlt = acc / cnt - cur
            m = _dot(dlt.astype(BF16), wg_ref[0, g]) * ps_ref[:, lo:lo + gw]
            mp_ref[0, r0:r0 + rp, lo:lo + gw] = m.astype(BF16)


def _convpool(z, z_halo_src, z_init, p, p_halo_src, p_init, w_dw, b_dw, ln_g, ln_b, w_grp, pool_scale,
              layer, tm, pos0):
    nb, tb, width = z.shape
    gw = width // len(POOL_WINDOWS)
    zsteps, psteps = tm // CONV_HALO, tm // POOL_HALO
    vec = pl.BlockSpec((1, width), lambda b, i: (0, 0))
    return pl.pallas_call(
        functools.partial(_convpool_kernel, tm=tm, pos0=pos0),
        out_shape=(jax.ShapeDtypeStruct((nb, tb, width), BF16), jax.ShapeDtypeStruct((nb, tb, width), BF16)),
        grid=(nb, tb // tm),
        in_specs=[
            pl.BlockSpec((1, tm, width), lambda b, i: (b, i, 0)),
            pl.BlockSpec((1, CONV_HALO, width), lambda b, i: (b, jnp.maximum(i * zsteps - 1, 0), 0)),
            pl.BlockSpec((1, CONV_HALO, width), lambda b, i: (b, 0, 0)),
            pl.BlockSpec((1, tm, width), lambda b, i: (b, i, 0)),
            pl.BlockSpec((1, POOL_HALO, width), lambda b, i: (b, jnp.maximum(i * psteps - 1, 0), 0)),
            pl.BlockSpec((1, POOL_HALO, width), lambda b, i: (b, 0, 0)),
            pl.BlockSpec((CONV_K, width), lambda b, i: (0, 0)),
            vec, vec, vec,
            pl.BlockSpec((1, len(POOL_WINDOWS), gw, gw), lambda b, i: (layer, 0, 0, 0)),
            vec,
        ],
        out_specs=(pl.BlockSpec((1, tm, width), lambda b, i: (b, i, 0)),
                   pl.BlockSpec((1, tm, width), lambda b, i: (b, i, 0))),
        scratch_shapes=[pltpu.VMEM((CONV_HALO + tm, width), F32),
                        pltpu.VMEM((SUBLANES, CONV_HALO + tm - SUBLANES, width), F32),
                        pltpu.VMEM((POOL_HALO + tm, width), F32)],
        compiler_params=_params(("parallel", "arbitrary")),
        name="convpool",
    )(z, z_halo_src, z_init, p, p_halo_src, p_init, w_dw, b_dw, ln_g, ln_b, w_grp, pool_scale)


def _attn_kernel(q_ref, k_ref, vt_ref, gs_ref, lam_ref, o_ref, qbd_s, s_s, m_s, acc_s, *, tq):
    kb = ATT_KB
    ks = ATT_KS
    gwid = 2 * kb
    ng = tq // kb
    assert tq == 2 * ks and ks == 2 * kb
    qi = pl.program_id(2)

    qt = q_ref[0].astype(F32).T
    row = lax.broadcasted_iota(jnp.int32, (V_DIM, kb), 0)
    for c in range(ng):
        blk = qt[:, c * kb:(c + 1) * kb]
        qbd_s[:, c * gwid:c * gwid + kb] = jnp.where(row < HEAD_DIM, blk, 0.0).astype(BF16)
        qbd_s[:, c * gwid + kb:(c + 1) * gwid] = jnp.where(row >= HEAD_DIM, blk, 0.0).astype(BF16)
    m_s[...] = jnp.full(m_s.shape, -jnp.inf, F32)
    acc_s[...] = jnp.zeros(acc_s.shape, F32)

    def scores(js, lo):
        keys = k_ref[0, pl.ds(pl.multiple_of(js * ks, ks), ks), :]
        return _dot(keys, qbd_s[:, lo:])

    def update(st, vt, lo):
        m_prev = m_s[:, lo:]
        m_new = jnp.maximum(m_prev, jnp.max(st, axis=0, keepdims=True))
        alpha = jnp.exp2(m_prev - m_new)
        pt = jnp.exp2(st - m_new).astype(BF16)
        acc_s[:, lo:] = acc_s[:, lo:] * alpha + _dot(vt, pt)
        m_s[:, lo:] = m_new

    n_full = 2 * qi
    s_s[0] = scores(0, 0)

    def pair(jj, carry):
        j0 = 2 * jj
        s_s[1] = scores(j0 + 1, 0)
        update(s_s[0], vt_ref[0, 0, j0], 0)
        s_s[0] = scores(j0 + 2, 0)
        update(s_s[1], vt_ref[0, 0, j0 + 1], 0)
        return carry

    lax.fori_loop(0, qi, pair, 0)

    krow = lax.broadcasted_iota(jnp.int32, (kb, kb), 0)
    qcol = lax.broadcasted_iota(jnp.int32, (kb, kb), 1)
    shift = CHUNK.bit_length() - 1
    vis = lax.shift_right_logical(krow, shift) <= lax.shift_right_logical(qcol, shift)

    def masked(st):
        parts = [jnp.where(vis, st[:, :kb], -jnp.inf), jnp.where(vis, st[:, kb:gwid], -jnp.inf)]
        if st.shape[1] > gwid:
            parts.append(st[:, gwid:])
        return jnp.concatenate(parts, axis=1)

    s_s[1, :, 2 * gwid:] = scores(n_full + 1, 2 * gwid)
    for s in range(ng):
        buf, r0, lo = s // 2, (s % 2) * kb, s * gwid
        update(masked(s_s[buf, r0:r0 + kb, lo:]), vt_ref[0, 0, n_full + buf, :, r0:r0 + kb], lo)

    lam = lam_ref[0, 0]
    for c in range(ng):
        lo = c * gwid
        o1 = acc_s[0:V_DIM, lo:lo + kb] / acc_s[V_DIM:V_DIM + 1, lo:lo + kb]
        o2 = acc_s[0:V_DIM, lo + kb:lo + gwid] / acc_s[V_DIM:V_DIM + 1, lo + kb:lo + gwid]
        ot = o1 - lam * o2
        ms = jnp.mean(ot * ot, axis=0, keepdims=True)
        ot = ot * lax.rsqrt(ms + EPS) * gs_ref[...]
        o_ref[0, c * kb:(c + 1) * kb, :] = ot.T.astype(BF16)


def _attn(qb, kb, vt, g_sub_col, lam):
    nb, t, width = qb.shape
    tq = 2 * ATT_KS
    assert t % tq == 0
    return pl.pallas_call(
        functools.partial(_attn_kernel, tq=tq),
        out_shape=jax.ShapeDtypeStruct((nb, t, width), BF16),
        grid=(nb, N_HEADS, t // tq),
        in_specs=[
            pl.BlockSpec((1, tq, V_DIM), lambda b, h, i: (b, i, h)),
            pl.BlockSpec((1, t, V_DIM), lambda b, h, i: (b, 0, h)),
            pl.BlockSpec((1, 1, t // ATT_KS, V_ROWS, ATT_KS), lambda b, h, i: (b, h, 0, 0, 0)),
            pl.BlockSpec((V_DIM, 1), lambda b, h, i: (0, 0)),
            pl.BlockSpec(memory_space=pltpu.SMEM),
        ],
        out_specs=pl.BlockSpec((1, tq, V_DIM), lambda b, h, i: (b, i, h)),
        scratch_shapes=[
            pltpu.VMEM((V_DIM, 2 * tq), BF16),
            pltpu.VMEM((2, ATT_KS, 2 * tq), F32),
            pltpu.VMEM((1, 2 * tq), F32),
            pltpu.VMEM((V_ROWS, 2 * tq), F32),
        ],
        compiler_params=_params(("parallel", "parallel", "arbitrary")),
        name="attn",
    )(qb, kb, vt, g_sub_col, lam)


def _attn_s_kernel(q_ref, kn_ref, vn_ref, kc_ref, vc_ref, gs_ref, lam_ref, o_ref, qbd_s, m_s, l_s, acc_s,
                   *, ts, tk, n_kv):
    j = pl.program_id(1)

    def fold(h, s, v_bf16, m_prev, l_prev, acc_prev):
        m_new = jnp.maximum(m_prev, jnp.max(s, axis=1, keepdims=True))
        alpha = jnp.exp2(m_prev - m_new)
        p = jnp.exp2(s - m_new)
        l_s[h] = alpha * l_prev + jnp.sum(p, axis=1, keepdims=True)
        acc_s[h] = alpha * acc_prev + _dot(p.astype(BF16), v_bf16)
        m_s[h] = m_new

    @pl.when(j == 0)
    def _init():
        lane = lax.broadcasted_iota(jnp.int32, (ts, V_DIM), 1)
        pad_col = lax.broadcasted_iota(jnp.int32, (2 * ts, LANES), 1) >= ts
        for h in range(N_HEADS):
            cols = slice(h * V_DIM, (h + 1) * V_DIM)
            q = q_ref[0, :, cols].astype(F32)
            qbd_s[h, 0:ts, :] = jnp.where(lane < HEAD_DIM, q, 0.0).astype(BF16)
            qbd_s[h, ts:2 * ts, :] = jnp.where(lane >= HEAD_DIM, q, 0.0).astype(BF16)
            s = _dot_nt(qbd_s[h], kn_ref[0, :, cols])
            s = jnp.where(pad_col, -jnp.inf, s)
            fold(h, s, vn_ref[0, :, cols], jnp.full((2 * ts, 1), -jnp.inf, F32), jnp.zeros((2 * ts, 1), F32),
                 jnp.zeros((2 * ts, V_DIM), F32))

    for h in range(N_HEADS):
        s = _dot(qbd_s[h], kc_ref[0, 0, h].astype(BF16))
        v_h = vc_ref[0, 0, pl.ds(h, tk, stride=N_HEADS), :].astype(BF16)
        fold(h, s, v_h, m_s[h], l_s[h], acc_s[h])

    @pl.when(j == n_kv - 1)
    def _finish():
        lam = lam_ref[0, 0]
        for h in range(N_HEADS):
            o1 = acc_s[h, 0:ts, :] / l_s[h, 0:ts, :]
            o2 = acc_s[h, ts:2 * ts, :] / l_s[h, ts:2 * ts, :]
            o = o1 - lam * o2
            ms = jnp.mean(o * o, axis=-1, keepdims=True)
            o_ref[0, :, h * V_DIM:(h + 1) * V_DIM] = (o * lax.rsqrt(ms + EPS) * gs_ref[...]).astype(BF16)


def _attn_s(qb, kb_new, v_new, cache_k, cache_v, g_sub_row, lam, layer, tk):
    nb, ts, width = qb.shape
    past = cache_k.shape[4]
    n_kv = past // tk
    pad = ((0, 0), (0, LANES - ts), (0, 0))
    kb_new = jnp.pad(kb_new, pad)
    v_new = jnp.pad(v_new.astype(BF16), pad)
    new = pl.BlockSpec((1, ts, width), lambda b, j: (b, 0, 0))
    new_kv = pl.BlockSpec((1, LANES, width), lambda b, j: (b, 0, 0))
    old_k = pl.BlockSpec((1, 1, N_HEADS, V_DIM, tk), lambda b, j: (layer, b, 0, 0, j))
    old_v = pl.BlockSpec((1, 1, tk * N_HEADS, V_DIM), lambda b, j: (layer, b, j, 0))
    return pl.pallas_call(
        functools.partial(_attn_s_kernel, ts=ts, tk=tk, n_kv=n_kv),
        out_shape=jax.ShapeDtypeStruct((nb, ts, width), BF16),
        grid=(nb, n_kv),
        in_specs=[new, new_kv, new_kv, old_k, old_v,
                  pl.BlockSpec((1, V_DIM), lambda b, j: (0, 0)),
                  pl.BlockSpec(memory_space=pltpu.SMEM)],
        out_specs=new,
        scratch_shapes=[
            pltpu.VMEM((N_HEADS, 2 * ts, V_DIM), BF16),
            pltpu.VMEM((N_HEADS, 2 * ts, 1), F32),
            pltpu.VMEM((N_HEADS, 2 * ts, 1), F32),
            pltpu.VMEM((N_HEADS, 2 * ts, V_DIM), F32),
        ],
        compiler_params=_params(("parallel", "arbitrary")),
        name="attn_s",
    )(qb, kb_new, v_new, cache_k, cache_v, g_sub_row, lam)


def _merge_kernel(x_ref, yc_ref, mp_ref, o_ref, g_ref, sh_ref, sc_ref, g2_ref,
                  wg_ref, wc_ref, wp_ref, wa_ref, wo_ref, out_ref):
    x = x_ref[0]
    d = x.shape[1]
    h = _rms_mod(x, g_ref[...], sh_ref[0], sc_ref[0]).astype(BF16)
    merged = _sigmoid(_dot(h, wg_ref[0, :, 0:d])) * _dot(yc_ref[0], wc_ref[0])
    merged = merged + _sigmoid(_dot(h, wg_ref[0, :, d:2 * d])) * _dot(mp_ref[0], wp_ref[0])
    merged = merged + _sigmoid(_dot(h, wg_ref[0, :, 2 * d:3 * d])) * _dot(o_ref[0], wa_ref[0])
    out_ref[0] = x + g2_ref[0] * _dot(merged.astype(BF16), wo_ref[0])


def _merge(x, yc, mp, o, g, sh, sc, g2, w_gate, w_conv_out, w_pool_out, w_attn_out, w_out, layer, tm):
    nb, tb, d = x.shape
    rspec = lambda width: pl.BlockSpec((1, tm, width), lambda b, i: (b, i, 0))
    wspec = lambda w: _resident((1,) + w.shape[1:], lambda b, i: (layer, 0, 0))
    return pl.pallas_call(
        _merge_kernel,
        out_shape=jax.ShapeDtypeStruct(x.shape, F32),
        grid=(nb, tb // tm),
        in_specs=[
            rspec(d), rspec(yc.shape[2]), rspec(mp.shape[2]), rspec(o.shape[2]),
            pl.BlockSpec((1, d), lambda b, i: (0, 0)),
            _row_spec(sh, tm), _row_spec(sc, tm), _row_spec(g2, tm),
            wspec(w_gate), wspec(w_conv_out), wspec(w_pool_out), wspec(w_attn_out), wspec(w_out),
        ],
        out_specs=rspec(d),
        compiler_params=_params(("parallel", "parallel")),
        name="merge",
    )(x, yc, mp, o, g, sh, sc, g2, w_gate, w_conv_out, w_pool_out, w_attn_out, w_out)


def _rope_tables(pos):
    half = HEAD_DIM // 2
    inv = ROPE_THETA ** (-jnp.arange(half, dtype=F32) / half)
    ang = pos.astype(F32)[:, None] * inv[None, :]
    cos, sin = jnp.cos(ang), jnp.sin(ang)
    return jnp.tile(cos, (1, 4)), jnp.concatenate([-sin, sin, -sin, sin], axis=-1)


def _tile_rows(total, want):
    t = min(want, total)
    assert total % t == 0, (total, t)
    return t


def kernel(x_prompt, x_sample, cache_attn_k, cache_attn_v, state_conv, state_pool, c_prompt, c_sample, w_ada, b_ada, g_ffn1, w_ffn1_in, w_ffn1_out, g_mix, w_in, w_dw, b_dw, ln_conv_g, ln_conv_b, w_conv_out, w_pool_grp, pool_scale, w_pool_out, g_q, g_k, lam_q1, lam_k1, lam_q2, lam_k2, g_sub, w_attn_out, w_out, g_ffn2, w_ffn2_in, w_ffn2_out):
    bp, tp, d = x_prompt.shape
    bs, ts, _ = x_sample.shape
    n_layers = w_ada.shape[0]
    past = cache_attn_k.shape[2]
    conv_w = w_dw.shape[2]
    pool_w = pool_scale.shape[1]
    attn_w = N_HEADS * V_DIM
    proj_w = 2 * conv_w + pool_w + 3 * attn_w
    assert CONV_HALO >= CONV_K - 1 and POOL_HALO >= POOL_BUF and tp >= CONV_HALO

    w_ffn1_in_b, w_ffn1_out_b = w_ffn1_in.astype(BF16), w_ffn1_out.astype(BF16)
    w_ffn2_in_b, w_ffn2_out_b = w_ffn2_in.astype(BF16), w_ffn2_out.astype(BF16)
    w_proj_b = w_in[:, :, :proj_w].astype(BF16)
    w_gate_b = w_in[:, :, proj_w:].astype(BF16)
    w_conv_out_b, w_pool_out_b = w_conv_out.astype(BF16), w_pool_out.astype(BF16)
    w_attn_out_b, w_out_b = w_attn_out.astype(BF16), w_out.astype(BF16)
    w_grp_b = w_pool_grp.astype(BF16)
    chunk_id = np.arange(attn_w) // HEAD_DIM
    bd = jnp.asarray(chunk_id[:, None] == chunk_id[None, :], BF16)

    rows = bp + bs
    rows_pad = -(-rows // 16) * 16
    c_all = jnp.concatenate([c_prompt, c_sample, jnp.zeros((rows_pad - rows, d), F32)], axis=0)
    mod = _ada(c_all, w_ada, b_ada).reshape(n_layers, rows_pad, N_MOD, d)

    cos_p, sin_p = _rope_tables(jnp.arange(tp))
    cos_s, sin_s = _rope_tables(past + jnp.arange(ts))
    cos_p, sin_p = cos_p[None], sin_p[None]
    cos_s, sin_s = jnp.tile(cos_s, (bs, 1))[None], jnp.tile(sin_s, (bs, 1))[None]

    ms = bs * ts
    cache_k = jnp.transpose(cache_attn_k, (0, 1, 3, 4, 5, 2)).reshape(n_layers, bs, N_HEADS, V_DIM, past)
    cache_v = cache_attn_v.reshape(n_layers, bs, past * N_HEADS, V_DIM)
    tk_s = _tile_rows(past, 2048)

    tm_p = _tile_rows(tp, 256)
    tm_cp = _tile_rows(tp, 512)
    q_scale = HEAD_DIM ** -0.5 * math.log2(math.e)

    xp = x_prompt
    xs = x_sample.reshape(1, ms, d)
    kv_p = kv_s = None
    outs = [[] for _ in range(4)]
    for l in range(n_layers):
        lam_init = 0.8 - 0.6 * math.exp(-0.3 * l)
        lam = (jnp.exp(jnp.sum(lam_q1[l] * lam_k1[l])) - jnp.exp(jnp.sum(lam_q2[l] * lam_k2[l])) + lam_init)
        lam = lam.reshape(1, 1).astype(F32)
        g_sub_eff = g_sub[l] * (1.0 - lam_init)
        gq_t = (jnp.tile(g_q[l], attn_w // HEAD_DIM) * q_scale)[None]
        gk_t = jnp.tile(g_k[l], attn_w // HEAD_DIM)[None]

        mod_p = [mod[l, :bp, k][:, None, :] for k in range(N_MOD)]
        mod_s = [jnp.repeat(mod[l, bp:rows, k], ts, axis=0)[None] for k in range(N_MOD)]

        def layer(x, m, tm, tm_c, cos, sin, conv_state, pool_state, pos0, kv_stacks, attend, want_vt):
            sh1, sc1, g1, sh2, sc2, g2, sh3, sc3, g3 = m
            tm_d = min(2 * tm, x.shape[1])
            x = _ffn(x, g_ffn1[l][None], sh1, sc1, g1, w_ffn1_in_b, w_ffn1_out_b, l, tm_d)
            z, p, k_stack, v_stack, qb, kb, *vt = _inproj(
                x, g_mix[l][None], sh2, sc2, w_proj_b, bd, gq_t, gk_t, cos, sin, l, n_layers, tm,
                conv_w, pool_w, kv_stacks, want_vt)
            nb = conv_state.shape[0]
            zb = z.reshape(nb, -1, conv_w)
            pb = p.reshape(nb, -1, pool_w)
            z_init = jnp.pad(conv_state, ((0, 0), (CONV_HALO - (CONV_K - 1), 0), (0, 0)))
            p_init = jnp.pad(pool_state, ((0, 0), (POOL_HALO - POOL_BUF, 0), (0, 0)))
            halo_ok = zb.shape[1] >= CONV_HALO
            yc, mp = _convpool(zb, zb if halo_ok else z_init, z_init, pb, pb if halo_ok else p_init, p_init,
                               w_dw[l], b_dw[l][None], ln_conv_g[l][None], ln_conv_b[l][None],
                               w_grp_b, pool_scale[l][None], l, tm_c, pos0)
            o = attend(qb, kb, v_stack, *vt)
            x = _merge(x, yc.reshape(x.shape[0], -1, conv_w), mp.reshape(x.shape[0], -1, pool_w), o,
                       g_mix[l][None], sh2, sc2, g2, w_gate_b, w_conv_out_b, w_pool_out_b, w_attn_out_b,
                       w_out_b, l, tm_d)
            x = _ffn(x, g_ffn2[l][None], sh3, sc3, g3, w_ffn2_in_b, w_ffn2_out_b, l, tm_d)
            new_conv = jnp.concatenate([conv_state, zb], axis=1)[:, -(CONV_K - 1):]
            new_pool = jnp.concatenate([pool_state, pb], axis=1)[:, -POOL_BUF:]
            return x, (k_stack, v_stack), new_conv, new_pool

        def attend_p(qb, kb, v_stack, vt):
            return _attn(qb, kb, vt, g_sub_eff.reshape(V_DIM, 1), lam)

        def attend_s(qb, kb, v_stack):
            o = _attn_s(qb.reshape(bs, ts, attn_w), kb.reshape(bs, ts, attn_w), v_stack[l].reshape(bs, ts, attn_w),
                        cache_k, cache_v, g_sub_eff.reshape(1, V_DIM), lam, l, tk_s)
            return o.reshape(1, ms, attn_w)

        xp, kv_p, cp, pp = layer(xp, mod_p, tm_p, tm_cp, cos_p, sin_p,
                                 jnp.zeros((bp, CONV_K - 1, conv_w), F32),
                                 jnp.zeros((bp, POOL_BUF, pool_w), F32), 0, kv_p, attend_p, True)
        xs, kv_s, cn, pn = layer(xs, mod_s, ms, ts, cos_s, sin_s, state_conv[l], state_pool[l], past,
                                 kv_s, attend_s, False)
        for lst, val in zip(outs, (cp, pp, cn, pn)):
            lst.append(val)

    cp, pp, cn, pn = [jnp.stack(o) for o in outs]
    return (xp, xs.reshape(bs, ts, d),
            kv_p[0].reshape(n_layers, bp, tp, N_HEADS, 2, HEAD_DIM), kv_p[1].reshape(n_layers, bp, tp, N_HEADS, V_DIM),
            cp, pp,
            kv_s[0].reshape(n_layers, bs, ts, N_HEADS, 2, HEAD_DIM), kv_s[1].reshape(n_layers, bs, ts, N_HEADS, V_DIM),
            cn, pn)
```

```python
import functools
import math

import numpy as np
import jax
import jax.numpy as jnp
from jax import lax
from jax.experimental import pallas as pl
from jax.experimental.pallas import tpu as pltpu

F32 = jnp.float32
BF16 = jnp.bfloat16

EPS = 1e-6
CHUNK = 64
N_HEADS = 8
HEAD_DIM = 64
V_DIM = 2 * HEAD_DIM
CONV_K = 31
POOL_WINDOWS = (2, 4, 8, 16)
POOL_BUF = 15
ROPE_THETA = 10000.0
N_MOD = 9

SUBLANES = 8
LANES = 128
CONV_HALO = 32
POOL_HALO = 16
ATT_KB = 256
ATT_KS = 2 * ATT_KB
V_ROWS = V_DIM + 16
V7X_VMEM_LIMIT = 56 * 1024 * 1024


def _params(semantics):
    return pltpu.CompilerParams(dimension_semantics=semantics, vmem_limit_bytes=V7X_VMEM_LIMIT)


def _dot(a, b):
    return jnp.dot(a, b, preferred_element_type=F32)


def _dot_nt(a, b):
    return lax.dot_general(a, b, (((1,), (1,)), ((), ())), preferred_element_type=F32)


def _sigmoid(x):
    return 1.0 / (1.0 + jnp.exp(-x))


def _rms_mod(x, g, sh, sc):
    ms = jnp.mean(x * x, axis=-1, keepdims=True)
    y = x * lax.rsqrt(ms + EPS) * g
    return y * (1.0 + sc) + sh


def _row_spec(arr, tm):
    d = arr.shape[-1]
    if arr.shape[1] == 1:
        return pl.BlockSpec((1, 1, d), lambda b, i: (b, 0, 0))
    return pl.BlockSpec((1, tm, d), lambda b, i: (b, i, 0))


def _resident(shape, index_map):
    return pl.BlockSpec(shape, index_map, pipeline_mode=pl.Buffered(1))


def _ada_kernel(c_ref, w_ref, b_ref, o_ref):
    c = c_ref[...]
    a = (c * _sigmoid(c)).astype(BF16)
    o_ref[0] = _dot(a, w_ref[0].astype(BF16)) + b_ref[0]


def _ada(c_all, w_ada, b_ada):
    n_layers, d, n = w_ada.shape
    rows = c_all.shape[0]
    tn = 1024
    return pl.pallas_call(
        _ada_kernel,
        out_shape=jax.ShapeDtypeStruct((n_layers, rows, n), F32),
        grid=(n_layers, n // tn),
        in_specs=[
            pl.BlockSpec((rows, d), lambda l, j: (0, 0)),
            pl.BlockSpec((1, d, tn), lambda l, j: (l, 0, j)),
            pl.BlockSpec((1, 1, tn), lambda l, j: (l, 0, j)),
        ],
        out_specs=pl.BlockSpec((1, rows, tn), lambda l, j: (l, 0, j)),
        compiler_params=_params(("parallel", "parallel")),
        name="ada",
    )(c_all, w_ada, b_ada.reshape(n_layers, 1, n))


def _ffn_kernel(x_ref, g_ref, sh_ref, sc_ref, gt_ref, win_ref, wout_ref, o_ref, *, d_ff):
    x = x_ref[0]
    h = _rms_mod(x, g_ref[...], sh_ref[0], sc_ref[0]).astype(BF16)
    a = _dot(h, win_ref[0, :, :d_ff])
    b = _dot(h, win_ref[0, :, d_ff:])
    hid = (a * _sigmoid(a) * b).astype(BF16)
    y = _dot(hid, wout_ref[0])
    o_ref[0] = x + 0.5 * gt_ref[0] * y


def _ffn(x, g, sh, sc, gate, w_in, w_out, layer, tm):
    nb, tb, d = x.shape
    d_ff = w_out.shape[1]
    return pl.pallas_call(
        functools.partial(_ffn_kernel, d_ff=d_ff),
        out_shape=jax.ShapeDtypeStruct(x.shape, F32),
        grid=(nb, tb // tm),
        in_specs=[
            pl.BlockSpec((1, tm, d), lambda b, i: (b, i, 0)),
            pl.BlockSpec((1, d), lambda b, i: (0, 0)),
            _row_spec(sh, tm), _row_spec(sc, tm), _row_spec(gate, tm),
            _resident((1, d, 2 * d_ff), lambda b, i: (layer, 0, 0)),
            _resident((1, d_ff, d), lambda b, i: (layer, 0, 0)),
        ],
        out_specs=pl.BlockSpec((1, tm, d), lambda b, i: (b, i, 0)),
        compiler_params=_params(("parallel", "parallel")),
        name="ffn",
    )(x, g, sh, sc, gate, w_in, w_out)


def _norm_rope(u, bd_ref, g_t, cos, sin, f32_ref, bf16_ref):
    ssq = _dot((u * u).astype(BF16), bd_ref[...])
    y = u * lax.rsqrt(ssq * (1.0 / HEAD_DIM) + EPS) * g_t
    tm = u.shape[0]
    lane = lax.broadcasted_iota(jnp.int32, (tm, LANES), 1)
    first_half = (lane & (HEAD_DIM - 1)) < HEAD_DIM // 2
    for h in range(u.shape[1] // LANES):
        yh = y[:, h * LANES:(h + 1) * LANES]
        partner = jnp.where(first_half,
                            pltpu.roll(yh, LANES - HEAD_DIM // 2, 1),
                            pltpu.roll(yh, HEAD_DIM // 2, 1))
        r = yh * cos + partner * sin
        if f32_ref is not None:
            f32_ref[0, :, h * LANES:(h + 1) * LANES] = r
        bf16_ref[0, :, h * LANES:(h + 1) * LANES] = r.astype(BF16)


def _inproj_kernel(*refs, conv_w, pool_w, attn_w, n_alias, want_vt):
    (x_ref, g_ref, sh_ref, sc_ref, w_ref, bd_ref, gq_ref, gk_ref, cos_ref, sin_ref) = refs[:10]
    outs = refs[10 + n_alias:]
    z_ref, p_ref, k_stack_ref, v_stack_ref, qb_ref, kb_ref = outs[:6]
    k_ref, v_ref = k_stack_ref.at[0], v_stack_ref.at[0]
    x = x_ref[0]
    h = _rms_mod(x, g_ref[...], sh_ref[0], sc_ref[0]).astype(BF16)
    o = 0
    a = _dot(h, w_ref[0, :, o:o + conv_w]); o += conv_w
    gt = _dot(h, w_ref[0, :, o:o + conv_w]); o += conv_w
    z_ref[0] = a * _sigmoid(gt)
    p_ref[0] = _dot(h, w_ref[0, :, o:o + pool_w]); o += pool_w
    cos = cos_ref[0]
    sin = sin_ref[0]
    uq = _dot(h, w_ref[0, :, o:o + attn_w]); o += attn_w
    _norm_rope(uq, bd_ref, gq_ref[...], cos, sin, None, qb_ref)
    uk = _dot(h, w_ref[0, :, o:o + attn_w]); o += attn_w
    _norm_rope(uk, bd_ref, gk_ref[...], cos, sin, k_ref, kb_ref)
    uv = _dot(h, w_ref[0, :, o:o + attn_w])
    v_ref[0] = uv
    if want_vt:
        vt_ref = outs[6]
        tail = lax.broadcasted_iota(jnp.int32, (V_ROWS - V_DIM, uv.shape[0]), 0)
        ones_rows = jnp.where(tail == 0, 1.0, 0.0).astype(BF16)
        for hd in range(attn_w // V_DIM):
            vt_ref[0, hd, 0, 0:V_DIM, :] = uv[:, hd * V_DIM:(hd + 1) * V_DIM].T.astype(BF16)
            vt_ref[0, hd, 0, V_DIM:V_ROWS, :] = ones_rows


def _inproj(x, g, sh, sc, w, bd, gq_t, gk_t, cos, sin, layer, n_layers, tm, conv_w, pool_w, kv_stacks, want_vt):
    nb, tb, d = x.shape
    attn_w = N_HEADS * V_DIM
    n = w.shape[2]
    row = lambda width, dt: jax.ShapeDtypeStruct((nb, tb, width), dt)
    rspec = lambda width: pl.BlockSpec((1, tm, width), lambda b, i: (b, i, 0))
    stack = jax.ShapeDtypeStruct((n_layers, nb, tb, attn_w), F32)
    stack_spec = pl.BlockSpec((1, 1, tm, attn_w), lambda b, i: (layer, b, i, 0))
    out_shape = [row(conv_w, F32), row(pool_w, F32), stack, stack, row(attn_w, BF16), row(attn_w, BF16)]
    out_specs = [rspec(conv_w), rspec(pool_w), stack_spec, stack_spec, rspec(attn_w), rspec(attn_w)]
    if want_vt:
        per = ATT_KS // tm
        assert ATT_KS % tm == 0 and tb % ATT_KS == 0 and tm % LANES == 0
        out_shape.append(jax.ShapeDtypeStruct((nb, N_HEADS, tb // ATT_KS, V_ROWS, ATT_KS), BF16))
        out_specs.append(pl.BlockSpec((1, N_HEADS, 1, V_ROWS, tm), lambda b, i: (b, 0, i // per, 0, i % per)))
    in_specs = [
        rspec(d),
        pl.BlockSpec((1, d), lambda b, i: (0, 0)),
        _row_spec(sh, tm), _row_spec(sc, tm),
        _resident((1, d, n), lambda b, i: (layer, 0, 0)),
        _resident((attn_w, attn_w), lambda b, i: (0, 0)),
        pl.BlockSpec((1, attn_w), lambda b, i: (0, 0)),
        pl.BlockSpec((1, attn_w), lambda b, i: (0, 0)),
        pl.BlockSpec((1, tm, LANES), lambda b, i: (0, i, 0)),
        pl.BlockSpec((1, tm, LANES), lambda b, i: (0, i, 0)),
    ]
    args = [x, g, sh, sc, w, bd, gq_t, gk_t, cos, sin]
    aliases = {}
    if kv_stacks is not None:
        aliases = {len(args): 2, len(args) + 1: 3}
        in_specs += [pl.BlockSpec(memory_space=pl.ANY)] * 2
        args += list(kv_stacks)
    return pl.pallas_call(
        functools.partial(_inproj_kernel, conv_w=conv_w, pool_w=pool_w, attn_w=attn_w,
                          n_alias=len(aliases), want_vt=want_vt),
        out_shape=tuple(out_shape),
        grid=(nb, tb // tm),
        in_specs=in_specs,
        out_specs=tuple(out_specs),
        input_output_aliases=aliases,
        compiler_params=_params(("parallel", "parallel")),
        name="inproj",
    )(*args)


def _convpool_kernel(z_ref, zh_ref, zi_ref, p_ref, ph_ref, pi_ref, wdw_ref, bdw_ref, lng_ref, lnb_ref,
                     wg_ref, ps_ref, yc_ref, mp_ref, zc_s, zsh_s, pc_s, *, tm, pos0):
    i = pl.program_id(1)
    first = i == 0
    zc_s[0:CONV_HALO, :] = jnp.where(first, zi_ref[0], zh_ref[0])
    zc_s[CONV_HALO:CONV_HALO + tm, :] = z_ref[0]
    pc_s[0:POOL_HALO, :] = jnp.where(first, pi_ref[0], ph_ref[0])
    pc_s[POOL_HALO:POOL_HALO + tm, :] = p_ref[0]
    n_sh = zsh_s.shape[1]
    for k in range(1, SUBLANES):
        zsh_s[k] = zc_s[k:k + n_sh, :]

    width = zc_s.shape[1]
    rc = min(32, tm)
    for c in range(tm // rc):
        r0 = c * rc
        acc = jnp.broadcast_to(bdw_ref[...], (rc, width))
        for j in range(CONV_K):
            off = r0 + CONV_HALO - (CONV_K - 1) + j
            k, base = off % SUBLANES, off - off % SUBLANES
            tap = zc_s[base:base + rc, :] if k == 0 else zsh_s[k, base:base + rc, :]
            acc = acc + wdw_ref[j:j + 1, :] * tap
        mu = jnp.mean(acc, axis=-1, keepdims=True)
        dlt = acc - mu
        var = jnp.mean(dlt * dlt, axis=-1, keepdims=True)
        yn = dlt * lax.rsqrt(var + EPS) * lng_ref[...] + lnb_ref[...]
        yc_ref[0, r0:r0 + rc, :] = (yn * _sigmoid(yn)).astype(BF16)

    gw = width // len(POOL_WINDOWS)
    rp = min(128, tm)
    for c in range(tm // rp):
        r0 = c * rp
        pos = lax.broadcasted_iota(jnp.int32, (rp, 1), 0) + (pos0 + i * tm + r0)
        for g, w in enumerate(POOL_WINDOWS):
            lo = g * gw
            cur = pc_s[POOL_HALO + r0:POOL_HALO + r0 + rp, lo:lo + gw]
            acc = cur
            for s in range(1, w):
                acc = acc + pc_s[POOL_HALO + r0 - s:POOL_HALO + r0 - s + rp, lo:lo + gw]
            cnt = jnp.minimum(w, pos + 1).astype(F32)
            dlt = acc / cnt - cur
            m = _dot(dlt.astype(BF16), wg_ref[0, g]) * ps_ref[:, lo:lo + gw]
            mp_ref[0, r0:r0 + rp, lo:lo + gw] = m.astype(BF16)


def _convpool(z, z_halo_src, z_init, p, p_halo_src, p_init, w_dw, b_dw, ln_g, ln_b, w_grp, pool_scale,
              layer, tm, pos0):
    nb, tb, width = z.shape
    gw = width // len(POOL_WINDOWS)
    zsteps, psteps = tm // CONV_HALO, tm // POOL_HALO
    vec = pl.BlockSpec((1, width), lambda b, i: (0, 0))
    return pl.pallas_call(
        functools.partial(_convpool_kernel, tm=tm, pos0=pos0),
        out_shape=(jax.ShapeDtypeStruct((nb, tb, width), BF16), jax.ShapeDtypeStruct((nb, tb, width), BF16)),
        grid=(nb, tb // tm),
        in_specs=[
            pl.BlockSpec((1, tm, width), lambda b, i: (b, i, 0)),
            pl.BlockSpec((1, CONV_HALO, width), lambda b, i: (b, jnp.maximum(i * zsteps - 1, 0), 0)),
            pl.BlockSpec((1, CONV_HALO, width), lambda b, i: (b, 0, 0)),
            pl.BlockSpec((1, tm, width), lambda b, i: (b, i, 0)),
            pl.BlockSpec((1, POOL_HALO, width), lambda b, i: (b, jnp.maximum(i * psteps - 1, 0), 0)),
            pl.BlockSpec((1, POOL_HALO, width), lambda b, i: (b, 0, 0)),
            pl.BlockSpec((CONV_K, width), lambda b, i: (0, 0)),
            vec, vec, vec,
            pl.BlockSpec((1, len(POOL_WINDOWS), gw, gw), lambda b, i: (layer, 0, 0, 0)),
            vec,
        ],
        out_specs=(pl.BlockSpec((1, tm, width), lambda b, i: (b, i, 0)),
                   pl.BlockSpec((1, tm, width), lambda b, i: (b, i, 0))),
        scratch_shapes=[pltpu.VMEM((CONV_HALO + tm, width), F32),
                        pltpu.VMEM((SUBLANES, CONV_HALO + tm - SUBLANES, width), F32),
                        pltpu.VMEM((POOL_HALO + tm, width), F32)],
        compiler_params=_params(("parallel", "arbitrary")),
        name="convpool",
    )(z, z_halo_src, z_init, p, p_halo_src, p_init, w_dw, b_dw, ln_g, ln_b, w_grp, pool_scale)


def _attn_kernel(q_ref, k_ref, vt_ref, gs_ref, lam_ref, o_ref, qbd_s, s_s, m_s, acc_s, *, tq):
    kb = ATT_KB
    ks = ATT_KS
    gwid = 2 * kb
    ng = tq // kb
    assert tq == 2 * ks and ks == 2 * kb
    qi = pl.program_id(2)

    qt = q_ref[0].astype(F32).T
    row = lax.broadcasted_iota(jnp.int32, (V_DIM, kb), 0)
    for c in range(ng):
        blk = qt[:, c * kb:(c + 1) * kb]
        qbd_s[:, c * gwid:c * gwid + kb] = jnp.where(row < HEAD_DIM, blk, 0.0).astype(BF16)
        qbd_s[:, c * gwid + kb:(c + 1) * gwid] = jnp.where(row >= HEAD_DIM, blk, 0.0).astype(BF16)
    m_s[...] = jnp.full(m_s.shape, -jnp.inf, F32)
    acc_s[...] = jnp.zeros(acc_s.shape, F32)

    def scores(js, lo):
        keys = k_ref[0, pl.ds(pl.multiple_of(js * ks, ks), ks), :]
        return _dot(keys, qbd_s[:, lo:])

    def update(st, vt, lo):
        m_prev = m_s[:, lo:]
        m_new = jnp.maximum(m_prev, jnp.max(st, axis=0, keepdims=True))
        alpha = jnp.exp2(m_prev - m_new)
        pt = jnp.exp2(st - m_new).astype(BF16)
        acc_s[:, lo:] = acc_s[:, lo:] * alpha + _dot(vt, pt)
        m_s[:, lo:] = m_new

    n_full = 2 * qi
    s_s[0] = scores(0, 0)

    def pair(jj, carry):
        j0 = 2 * jj
        s_s[1] = scores(j0 + 1, 0)
        update(s_s[0], vt_ref[0, 0, j0], 0)
        s_s[0] = scores(j0 + 2, 0)
        update(s_s[1], vt_ref[0, 0, j0 + 1], 0)
        return carry

    def quad(jj, carry):
        pair(2 * jj, carry)
        return pair(2 * jj + 1, carry)

    lax.fori_loop(0, qi // 2, quad, 0)

    @pl.when(qi % 2 == 1)
    def _odd_pair():
        pair(qi - 1, 0)

    krow = lax.broadcasted_iota(jnp.int32, (kb, kb), 0)
    qcol = lax.broadcasted_iota(jnp.int32, (kb, kb), 1)
    shift = CHUNK.bit_length() - 1
    vis = lax.shift_right_logical(krow, shift) <= lax.shift_right_logical(qcol, shift)

    def masked(st):
        parts = [jnp.where(vis, st[:, :kb], -jnp.inf), jnp.where(vis, st[:, kb:gwid], -jnp.inf)]
        if st.shape[1] > gwid:
            parts.append(st[:, gwid:])
        return jnp.concatenate(parts, axis=1)

    s_s[1, :, 2 * gwid:] = scores(n_full + 1, 2 * gwid)
    for s in range(ng):
        buf, r0, lo = s // 2, (s % 2) * kb, s * gwid
        update(masked(s_s[buf, r0:r0 + kb, lo:]), vt_ref[0, 0, n_full + buf, :, r0:r0 + kb], lo)

    lam = lam_ref[0, 0]
    for c in range(ng):
        lo = c * gwid
        o1 = acc_s[0:V_DIM, lo:lo + kb] / acc_s[V_DIM:V_DIM + 1, lo:lo + kb]
        o2 = acc_s[0:V_DIM, lo + kb:lo + gwid] / acc_s[V_DIM:V_DIM + 1, lo + kb:lo + gwid]
        ot = o1 - lam * o2
        ms = jnp.mean(ot * ot, axis=0, keepdims=True)
        ot = ot * lax.rsqrt(ms + EPS) * gs_ref[...]
        o_ref[0, c * kb:(c + 1) * kb, :] = ot.T.astype(BF16)


def _attn(qb, kb, vt, g_sub_col, lam):
    nb, t, width = qb.shape
    tq = 2 * ATT_KS
    assert t % tq == 0
    return pl.pallas_call(
        functools.partial(_attn_kernel, tq=tq),
        out_shape=jax.ShapeDtypeStruct((nb, t, width), BF16),
        grid=(nb, N_HEADS, t // tq),
        in_specs=[
            pl.BlockSpec((1, tq, V_DIM), lambda b, h, i: (b, i, h)),
            pl.BlockSpec((1, t, V_DIM), lambda b, h, i: (b, 0, h)),
            pl.BlockSpec((1, 1, t // ATT_KS, V_ROWS, ATT_KS), lambda b, h, i: (b, h, 0, 0, 0)),
            pl.BlockSpec((V_DIM, 1), lambda b, h, i: (0, 0)),
            pl.BlockSpec(memory_space=pltpu.SMEM),
        ],
        out_specs=pl.BlockSpec((1, tq, V_DIM), lambda b, h, i: (b, i, h)),
        scratch_shapes=[
            pltpu.VMEM((V_DIM, 2 * tq), BF16),
            pltpu.VMEM((2, ATT_KS, 2 * tq), F32),
            pltpu.VMEM((1, 2 * tq), F32),
            pltpu.VMEM((V_ROWS, 2 * tq), F32),
        ],
        compiler_params=_params(("parallel", "parallel", "arbitrary")),
        name="attn",
    )(qb, kb, vt, g_sub_col, lam)


def _attn_s_kernel(q_ref, kn_ref, vn_ref, kc_ref, vc_ref, gs_ref, lam_ref, o_ref, qbd_s, m_s, l_s, acc_s,
                   *, ts, tk, n_kv):
    j = pl.program_id(1)

    def fold(h, s, v_bf16, m_prev, l_prev, acc_prev):
        m_new = jnp.maximum(m_prev, jnp.max(s, axis=1, keepdims=True))
        alpha = jnp.exp2(m_prev - m_new)
        p = jnp.exp2(s - m_new)
        l_s[h] = alpha * l_prev + jnp.sum(p, axis=1, keepdims=True)
        acc_s[h] = alpha * acc_prev + _dot(p.astype(BF16), v_bf16)
        m_s[h] = m_new

    @pl.when(j == 0)
    def _init():
        lane = lax.broadcasted_iota(jnp.int32, (ts, V_DIM), 1)
        pad_col = lax.broadcasted_iota(jnp.int32, (2 * ts, LANES), 1) >= ts
        for h in range(N_HEADS):
            cols = slice(h * V_DIM, (h + 1) * V_DIM)
            q = q_ref[0, :, cols].astype(F32)
            qbd_s[h, 0:ts, :] = jnp.where(lane < HEAD_DIM, q, 0.0).astype(BF16)
            qbd_s[h, ts:2 * ts, :] = jnp.where(lane >= HEAD_DIM, q, 0.0).astype(BF16)
            s = _dot_nt(qbd_s[h], kn_ref[0, :, cols])
            s = jnp.where(pad_col, -jnp.inf, s)
            fold(h, s, vn_ref[0, :, cols], jnp.full((2 * ts, 1), -jnp.inf, F32), jnp.zeros((2 * ts, 1), F32),
                 jnp.zeros((2 * ts, V_DIM), F32))

    for h in range(N_HEADS):
        s = _dot(qbd_s[h], kc_ref[0, 0, h].astype(BF16))
        v_h = vc_ref[0, 0, pl.ds(h, tk, stride=N_HEADS), :].astype(BF16)
        fold(h, s, v_h, m_s[h], l_s[h], acc_s[h])

    @pl.when(j == n_kv - 1)
    def _finish():
        lam = lam_ref[0, 0]
        for h in range(N_HEADS):
            o1 = acc_s[h, 0:ts, :] / l_s[h, 0:ts, :]
            o2 = acc_s[h, ts:2 * ts, :] / l_s[h, ts:2 * ts, :]
            o = o1 - lam * o2
            ms = jnp.mean(o * o, axis=-1, keepdims=True)
            o_ref[0, :, h * V_DIM:(h + 1) * V_DIM] = (o * lax.rsqrt(ms + EPS) * gs_ref[...]).astype(BF16)


def _attn_s(qb, kb_new, v_new, cache_k, cache_v, g_sub_row, lam, layer, tk):
    nb, ts, width = qb.shape
    past = cache_k.shape[4]
    n_kv = past // tk
    pad = ((0, 0), (0, LANES - ts), (0, 0))
    kb_new = jnp.pad(kb_new, pad)
    v_new = jnp.pad(v_new.astype(BF16), pad)
    new = pl.BlockSpec((1, ts, width), lambda b, j: (b, 0, 0))
    new_kv = pl.BlockSpec((1, LANES, width), lambda b, j: (b, 0, 0))
    old_k = pl.BlockSpec((1, 1, N_HEADS, V_DIM, tk), lambda b, j: (layer, b, 0, 0, j))
    old_v = pl.BlockSpec((1, 1, tk * N_HEADS, V_DIM), lambda b, j: (layer, b, j, 0))
    return pl.pallas_call(
        functools.partial(_attn_s_kernel, ts=ts, tk=tk, n_kv=n_kv),
        out_shape=jax.ShapeDtypeStruct((nb, ts, width), BF16),
        grid=(nb, n_kv),
        in_specs=[new, new_kv, new_kv, old_k, old_v,
                  pl.BlockSpec((1, V_DIM), lambda b, j: (0, 0)),
                  pl.BlockSpec(memory_space=pltpu.SMEM)],
        out_specs=new,
        scratch_shapes=[
            pltpu.VMEM((N_HEADS, 2 * ts, V_DIM), BF16),
            pltpu.VMEM((N_HEADS, 2 * ts, 1), F32),
            pltpu.VMEM((N_HEADS, 2 * ts, 1), F32),
            pltpu.VMEM((N_HEADS, 2 * ts, V_DIM), F32),
        ],
        compiler_params=_params(("parallel", "arbitrary")),
        name="attn_s",
    )(qb, kb_new, v_new, cache_k, cache_v, g_sub_row, lam)


def _merge_kernel(x_ref, yc_ref, mp_ref, o_ref, g_ref, sh_ref, sc_ref, g2_ref,
                  wg_ref, wc_ref, wp_ref, wa_ref, wo_ref, out_ref):
    x = x_ref[0]
    d = x.shape[1]
    h = _rms_mod(x, g_ref[...], sh_ref[0], sc_ref[0]).astype(BF16)
    merged = _sigmoid(_dot(h, wg_ref[0, :, 0:d])) * _dot(yc_ref[0], wc_ref[0])
    merged = merged + _sigmoid(_dot(h, wg_ref[0, :, d:2 * d])) * _dot(mp_ref[0], wp_ref[0])
    merged = merged + _sigmoid(_dot(h, wg_ref[0, :, 2 * d:3 * d])) * _dot(o_ref[0], wa_ref[0])
    out_ref[0] = x + g2_ref[0] * _dot(merged.astype(BF16), wo_ref[0])


def _merge(x, yc, mp, o, g, sh, sc, g2, w_gate, w_conv_out, w_pool_out, w_attn_out, w_out, layer, tm):
    nb, tb, d = x.shape
    rspec = lambda width: pl.BlockSpec((1, tm, width), lambda b, i: (b, i, 0))
    wspec = lambda w: _resident((1,) + w.shape[1:], lambda b, i: (layer, 0, 0))
    return pl.pallas_call(
        _merge_kernel,
        out_shape=jax.ShapeDtypeStruct(x.shape, F32),
        grid=(nb, tb // tm),
        in_specs=[
            rspec(d), rspec(yc.shape[2]), rspec(mp.shape[2]), rspec(o.shape[2]),
            pl.BlockSpec((1, d), lambda b, i: (0, 0)),
            _row_spec(sh, tm), _row_spec(sc, tm), _row_spec(g2, tm),
            wspec(w_gate), wspec(w_conv_out), wspec(w_pool_out), wspec(w_attn_out), wspec(w_out),
        ],
        out_specs=rspec(d),
        compiler_params=_params(("parallel", "parallel")),
        name="merge",
    )(x, yc, mp, o, g, sh, sc, g2, w_gate, w_conv_out, w_pool_out, w_attn_out, w_out)


def _rope_tables(pos):
    half = HEAD_DIM // 2
    inv = ROPE_THETA ** (-jnp.arange(half, dtype=F32) / half)
    ang = pos.astype(F32)[:, None] * inv[None, :]
    cos, sin = jnp.cos(ang), jnp.sin(ang)
    return jnp.tile(cos, (1, 4)), jnp.concatenate([-sin, sin, -sin, sin], axis=-1)


def _tile_rows(total, want):
    t = min(want, total)
    assert total % t == 0, (total, t)
    return t


def kernel(x_prompt, x_sample, cache_attn_k, cache_attn_v, state_conv, state_pool, c_prompt, c_sample, w_ada, b_ada, g_ffn1, w_ffn1_in, w_ffn1_out, g_mix, w_in, w_dw, b_dw, ln_conv_g, ln_conv_b, w_conv_out, w_pool_grp, pool_scale, w_pool_out, g_q, g_k, lam_q1, lam_k1, lam_q2, lam_k2, g_sub, w_attn_out, w_out, g_ffn2, w_ffn2_in, w_ffn2_out):
    bp, tp, d = x_prompt.shape
    bs, ts, _ = x_sample.shape
    n_layers = w_ada.shape[0]
    past = cache_attn_k.shape[2]
    conv_w = w_dw.shape[2]
    pool_w = pool_scale.shape[1]
    attn_w = N_HEADS * V_DIM
    proj_w = 2 * conv_w + pool_w + 3 * attn_w
    assert CONV_HALO >= CONV_K - 1 and POOL_HALO >= POOL_BUF and tp >= CONV_HALO

    w_ffn1_in_b, w_ffn1_out_b = w_ffn1_in.astype(BF16), w_ffn1_out.astype(BF16)
    w_ffn2_in_b, w_ffn2_out_b = w_ffn2_in.astype(BF16), w_ffn2_out.astype(BF16)
    w_proj_b = w_in[:, :, :proj_w].astype(BF16)
    w_gate_b = w_in[:, :, proj_w:].astype(BF16)
    w_conv_out_b, w_pool_out_b = w_conv_out.astype(BF16), w_pool_out.astype(BF16)
    w_attn_out_b, w_out_b = w_attn_out.astype(BF16), w_out.astype(BF16)
    w_grp_b = w_pool_grp.astype(BF16)
    chunk_id = np.arange(attn_w) // HEAD_DIM
    bd = jnp.asarray(chunk_id[:, None] == chunk_id[None, :], BF16)

    rows = bp + bs
    rows_pad = -(-rows // 16) * 16
    c_all = jnp.concatenate([c_prompt, c_sample, jnp.zeros((rows_pad - rows, d), F32)], axis=0)
    mod = _ada(c_all, w_ada, b_ada).reshape(n_layers, rows_pad, N_MOD, d)

    cos_p, sin_p = _rope_tables(jnp.arange(tp))
    cos_s, sin_s = _rope_tables(past + jnp.arange(ts))
    cos_p, sin_p = cos_p[None], sin_p[None]
    cos_s, sin_s = jnp.tile(cos_s, (bs, 1))[None], jnp.tile(sin_s, (bs, 1))[None]

    ms = bs * ts
    cache_k = jnp.transpose(cache_attn_k, (0, 1, 3, 4, 5, 2)).reshape(n_layers, bs, N_HEADS, V_DIM, past)
    cache_v = cache_attn_v.reshape(n_layers, bs, past * N_HEADS, V_DIM)
    tk_s = _tile_rows(past, 2048)

    tm_p = _tile_rows(tp, 512)
    tm_cp = _tile_rows(tp, 512)
    q_scale = HEAD_DIM ** -0.5 * math.log2(math.e)

    xp = x_prompt
    xs = x_sample.reshape(1, ms, d)
    kv_p = kv_s = None
    outs = [[] for _ in range(4)]
    for l in range(n_layers):
        lam_init = 0.8 - 0.6 * math.exp(-0.3 * l)
        lam = (jnp.exp(jnp.sum(lam_q1[l] * lam_k1[l])) - jnp.exp(jnp.sum(lam_q2[l] * lam_k2[l])) + lam_init)
        lam = lam.reshape(1, 1).astype(F32)
        g_sub_eff = g_sub[l] * (1.0 - lam_init)
        gq_t = (jnp.tile(g_q[l], attn_w // HEAD_DIM) * q_scale)[None]
        gk_t = jnp.tile(g_k[l], attn_w // HEAD_DIM)[None]

        mod_p = [mod[l, :bp, k][:, None, :] for k in range(N_MOD)]
        mod_s = [jnp.repeat(mod[l, bp:rows, k], ts, axis=0)[None] for k in range(N_MOD)]

        def layer(x, m, tm, tm_c, cos, sin, conv_state, pool_state, pos0, kv_stacks, attend, want_vt):
            sh1, sc1, g1, sh2, sc2, g2, sh3, sc3, g3 = m
            tm_d = tm
            x = _ffn(x, g_ffn1[l][None], sh1, sc1, g1, w_ffn1_in_b, w_ffn1_out_b, l, tm_d)
            z, p, k_stack, v_stack, qb, kb, *vt = _inproj(
                x, g_mix[l][None], sh2, sc2, w_proj_b, bd, gq_t, gk_t, cos, sin, l, n_layers, tm,
                conv_w, pool_w, kv_stacks, want_vt)
            nb = conv_state.shape[0]
            zb = z.reshape(nb, -1, conv_w)
            pb = p.reshape(nb, -1, pool_w)
            z_init = jnp.pad(conv_state, ((0, 0), (CONV_HALO - (CONV_K - 1), 0), (0, 0)))
            p_init = jnp.pad(pool_state, ((0, 0), (POOL_HALO - POOL_BUF, 0), (0, 0)))
            halo_ok = zb.shape[1] >= CONV_HALO
            yc, mp = _convpool(zb, zb if halo_ok else z_init, z_init, pb, pb if halo_ok else p_init, p_init,
                               w_dw[l], b_dw[l][None], ln_conv_g[l][None], ln_conv_b[l][None],
                               w_grp_b, pool_scale[l][None], l, tm_c, pos0)
            o = attend(qb, kb, v_stack, *vt)
            x = _merge(x, yc.reshape(x.shape[0], -1, conv_w), mp.reshape(x.shape[0], -1, pool_w), o,
                       g_mix[l][None], sh2, sc2, g2, w_gate_b, w_conv_out_b, w_pool_out_b, w_attn_out_b,
                       w_out_b, l, tm_d)
            x = _ffn(x, g_ffn2[l][None], sh3, sc3, g3, w_ffn2_in_b, w_ffn2_out_b, l, tm_d)
            new_conv = jnp.concatenate([conv_state, zb], axis=1)[:, -(CONV_K - 1):]
            new_pool = jnp.concatenate([pool_state, pb], axis=1)[:, -POOL_BUF:]
            return x, (k_stack, v_stack), new_conv, new_pool

        def attend_p(qb, kb, v_stack, vt):
            return _attn(qb, kb, vt, g_sub_eff.reshape(V_DIM, 1), lam)

        def attend_s(qb, kb, v_stack):
            o = _attn_s(qb.reshape(bs, ts, attn_w), kb.reshape(bs, ts, attn_w), v_stack[l].reshape(bs, ts, attn_w),
                        cache_k, cache_v, g_sub_eff.reshape(1, V_DIM), lam, l, tk_s)
            return o.reshape(1, ms, attn_w)

        xp, kv_p, cp, pp = layer(xp, mod_p, tm_p, tm_cp, cos_p, sin_p,
                                 jnp.zeros((bp, CONV_K - 1, conv_w), F32),
                                 jnp.zeros((bp, POOL_BUF, pool_w), F32), 0, kv_p, attend_p, True)
        xs, kv_s, cn, pn = layer(xs, mod_s, ms, ts, cos_s, sin_s, state_conv[l], state_pool[l], past,
                                 kv_s, attend_s, False)
        for lst, val in zip(outs, (cp, pp, cn, pn)):
            lst.append(val)

    cp, pp, cn, pn = [jnp.stack(o) for o in outs]
    return (xp, xs.reshape(bs, ts, d),
            kv_p[0].reshape(n_layers, bp, tp, N_HEADS, 2, HEAD_DIM), kv_p[1].reshape(n_layers, bp, tp, N_HEADS, V_DIM),
            cp, pp,
            kv_s[0].reshape(n_layers, bs, ts, N_HEADS, 2, HEAD_DIM), kv_s[1].reshape(n_layers, bs, ts, N_HEADS, V_DIM),
            cn, pn)
```

```python
import functools
import math

import numpy as np
import jax
import jax.numpy as jnp
from jax import lax
from jax.experimental import pallas as pl
from jax.experimental.pallas import tpu as pltpu

F32 = jnp.float32
BF16 = jnp.bfloat16

EPS = 1e-6
CHUNK = 64
N_HEADS = 8
HEAD_DIM = 64
V_DIM = 2 * HEAD_DIM
CONV_K = 31
POOL_WINDOWS = (2, 4, 8, 16)
POOL_BUF = 15
ROPE_THETA = 10000.0
N_MOD = 9

SUBLANES = 8
LANES = 128
CONV_HALO = 32
POOL_HALO = 16
ATT_KB = 256
ATT_KS = 2 * ATT_KB
V_ROWS = V_DIM + 16
V7X_VMEM_LIMIT = 56 * 1024 * 1024


def _params(semantics):
    return pltpu.CompilerParams(dimension_semantics=semantics, vmem_limit_bytes=V7X_VMEM_LIMIT)


def _dot(a, b):
    return jnp.dot(a, b, preferred_element_type=F32)


def _dot_nt(a, b):
    return lax.dot_general(a, b, (((1,), (1,)), ((), ())), preferred_element_type=F32)


def _sigmoid(x):
    return 1.0 / (1.0 + jnp.exp(-x))


def _rms_mod(x, g, sh, sc):
    ms = jnp.mean(x * x, axis=-1, keepdims=True)
    y = x * lax.rsqrt(ms + EPS) * g
    return y * (1.0 + sc) + sh


def _row_spec(arr, tm):
    d = arr.shape[-1]
    if arr.shape[1] == 1:
        return pl.BlockSpec((1, 1, d), lambda b, i: (b, 0, 0))
    return pl.BlockSpec((1, tm, d), lambda b, i: (b, i, 0))


def _resident(shape, index_map):
    return pl.BlockSpec(shape, index_map, pipeline_mode=pl.Buffered(1))


def _ada_kernel(c_ref, w_ref, b_ref, o_ref):
    c = c_ref[...]
    a = (c * _sigmoid(c)).astype(BF16)
    o_ref[0] = _dot(a, w_ref[0].astype(BF16)) + b_ref[0]


def _ada(c_all, w_ada, b_ada):
    n_layers, d, n = w_ada.shape
    rows = c_all.shape[0]
    tn = 1024
    return pl.pallas_call(
        _ada_kernel,
        out_shape=jax.ShapeDtypeStruct((n_layers, rows, n), F32),
        grid=(n_layers, n // tn),
        in_specs=[
            pl.BlockSpec((rows, d), lambda l, j: (0, 0)),
            pl.BlockSpec((1, d, tn), lambda l, j: (l, 0, j)),
            pl.BlockSpec((1, 1, tn), lambda l, j: (l, 0, j)),
        ],
        out_specs=pl.BlockSpec((1, rows, tn), lambda l, j: (l, 0, j)),
        compiler_params=_params(("parallel", "parallel")),
        name="ada",
    )(c_all, w_ada, b_ada.reshape(n_layers, 1, n))


def _ffn_kernel(x_ref, g_ref, sh_ref, sc_ref, gt_ref, win_ref, wout_ref, o_ref, *, d_ff):
    x = x_ref[0]
    h = _rms_mod(x, g_ref[...], sh_ref[0], sc_ref[0]).astype(BF16)
    a = _dot(h, win_ref[0, :, :d_ff])
    b = _dot(h, win_ref[0, :, d_ff:])
    hid = (a * _sigmoid(a) * b).astype(BF16)
    y = _dot(hid, wout_ref[0])
    o_ref[0] = x + 0.5 * gt_ref[0] * y


def _ffn(x, g, sh, sc, gate, w_in, w_out, layer, tm):
    nb, tb, d = x.shape
    d_ff = w_out.shape[1]
    return pl.pallas_call(
        functools.partial(_ffn_kernel, d_ff=d_ff),
        out_shape=jax.ShapeDtypeStruct(x.shape, F32),
        grid=(nb, tb // tm),
        in_specs=[
            pl.BlockSpec((1, tm, d), lambda b, i: (b, i, 0)),
            pl.BlockSpec((1, d), lambda b, i: (0, 0)),
            _row_spec(sh, tm), _row_spec(sc, tm), _row_spec(gate, tm),
            _resident((1, d, 2 * d_ff), lambda b, i: (layer, 0, 0)),
            _resident((1, d_ff, d), lambda b, i: (layer, 0, 0)),
        ],
        out_specs=pl.BlockSpec((1, tm, d), lambda b, i: (b, i, 0)),
        compiler_params=_params(("parallel", "parallel")),
        name="ffn",
    )(x, g, sh, sc, gate, w_in, w_out)


def _norm_rope(u, bd_ref, g_t, cos, sin, f32_ref, bf16_ref):
    ssq = _dot((u * u).astype(BF16), bd_ref[...])
    y = u * lax.rsqrt(ssq * (1.0 / HEAD_DIM) + EPS) * g_t
    tm = u.shape[0]
    lane = lax.broadcasted_iota(jnp.int32, (tm, LANES), 1)
    first_half = (lane & (HEAD_DIM - 1)) < HEAD_DIM // 2
    for h in range(u.shape[1] // LANES):
        yh = y[:, h * LANES:(h + 1) * LANES]
        partner = jnp.where(first_half,
                            pltpu.roll(yh, LANES - HEAD_DIM // 2, 1),
                            pltpu.roll(yh, HEAD_DIM // 2, 1))
        r = yh * cos + partner * sin
        if f32_ref is not None:
            f32_ref[0, :, h * LANES:(h + 1) * LANES] = r
        bf16_ref[0, :, h * LANES:(h + 1) * LANES] = r.astype(BF16)


def _inproj_kernel(*refs, conv_w, pool_w, attn_w, n_alias, want_vt):
    (x_ref, g_ref, sh_ref, sc_ref, w_ref, bd_ref, gq_ref, gk_ref, cos_ref, sin_ref) = refs[:10]
    outs = refs[10 + n_alias:]
    z_ref, p_ref, k_stack_ref, v_stack_ref, qb_ref, kb_ref = outs[:6]
    k_ref, v_ref = k_stack_ref.at[0], v_stack_ref.at[0]
    x = x_ref[0]
    h = _rms_mod(x, g_ref[...], sh_ref[0], sc_ref[0]).astype(BF16)
    o = 0
    a = _dot(h, w_ref[0, :, o:o + conv_w]); o += conv_w
    gt = _dot(h, w_ref[0, :, o:o + conv_w]); o += conv_w
    z_ref[0] = a * _sigmoid(gt)
    p_ref[0] = _dot(h, w_ref[0, :, o:o + pool_w]); o += pool_w
    cos = cos_ref[0]
    sin = sin_ref[0]
    uq = _dot(h, w_ref[0, :, o:o + attn_w]); o += attn_w
    _norm_rope(uq, bd_ref, gq_ref[...], cos, sin, None, qb_ref)
    uk = _dot(h, w_ref[0, :, o:o + attn_w]); o += attn_w
    _norm_rope(uk, bd_ref, gk_ref[...], cos, sin, k_ref, kb_ref)
    uv = _dot(h, w_ref[0, :, o:o + attn_w])
    v_ref[0] = uv
    if want_vt:
        vt_ref = outs[6]
        tail = lax.broadcasted_iota(jnp.int32, (V_ROWS - V_DIM, uv.shape[0]), 0)
        ones_rows = jnp.where(tail == 0, 1.0, 0.0).astype(BF16)
        for hd in range(attn_w // V_DIM):
            vt_ref[0, hd, 0, 0:V_DIM, :] = uv[:, hd * V_DIM:(hd + 1) * V_DIM].T.astype(BF16)
            vt_ref[0, hd, 0, V_DIM:V_ROWS, :] = ones_rows


def _inproj(x, g, sh, sc, w, bd, gq_t, gk_t, cos, sin, layer, n_layers, tm, conv_w, pool_w, kv_stacks, want_vt):
    nb, tb, d = x.shape
    attn_w = N_HEADS * V_DIM
    n = w.shape[2]
    row = lambda width, dt: jax.ShapeDtypeStruct((nb, tb, width), dt)
    rspec = lambda width: pl.BlockSpec((1, tm, width), lambda b, i: (b, i, 0))
    stack = jax.ShapeDtypeStruct((n_layers, nb, tb, attn_w), F32)
    stack_spec = pl.BlockSpec((1, 1, tm, attn_w), lambda b, i: (layer, b, i, 0))
    out_shape = [row(conv_w, F32), row(pool_w, F32), stack, stack, row(attn_w, BF16), row(attn_w, BF16)]
    out_specs = [rspec(conv_w), rspec(pool_w), stack_spec, stack_spec, rspec(attn_w), rspec(attn_w)]
    if want_vt:
        per = ATT_KS // tm
        assert ATT_KS % tm == 0 and tb % ATT_KS == 0 and tm % LANES == 0
        out_shape.append(jax.ShapeDtypeStruct((nb, N_HEADS, tb // ATT_KS, V_ROWS, ATT_KS), BF16))
        out_specs.append(pl.BlockSpec((1, N_HEADS, 1, V_ROWS, tm), lambda b, i: (b, 0, i // per, 0, i % per)))
    in_specs = [
        rspec(d),
        pl.BlockSpec((1, d), lambda b, i: (0, 0)),
        _row_spec(sh, tm), _row_spec(sc, tm),
        _resident((1, d, n), lambda b, i: (layer, 0, 0)),
        _resident((attn_w, attn_w), lambda b, i: (0, 0)),
        pl.BlockSpec((1, attn_w), lambda b, i: (0, 0)),
        pl.BlockSpec((1, attn_w), lambda b, i: (0, 0)),
        pl.BlockSpec((1, tm, LANES), lambda b, i: (0, i, 0)),
        pl.BlockSpec((1, tm, LANES), lambda b, i: (0, i, 0)),
    ]
    args = [x, g, sh, sc, w, bd, gq_t, gk_t, cos, sin]
    aliases = {len(args): 2, len(args) + 1: 3}
    in_specs += [pl.BlockSpec(memory_space=pl.ANY)] * 2
    args += list(kv_stacks)
    return pl.pallas_call(
        functools.partial(_inproj_kernel, conv_w=conv_w, pool_w=pool_w, attn_w=attn_w,
                          n_alias=len(aliases), want_vt=want_vt),
        out_shape=tuple(out_shape),
        grid=(nb, tb // tm),
        in_specs=in_specs,
        out_specs=tuple(out_specs),
        input_output_aliases=aliases,
        compiler_params=_params(("parallel", "parallel")),
        name="inproj",
    )(*args)


def _convpool_kernel(z_ref, zh_ref, zi_ref, p_ref, ph_ref, pi_ref, wdw_ref, bdw_ref, lng_ref, lnb_ref,
                     wg_ref, ps_ref, yc_ref, mp_ref, zc_s, zsh_s, pc_s, *, tm, pos0):
    i = pl.program_id(1)
    first = i == 0
    zc_s[0:CONV_HALO, :] = jnp.where(first, zi_ref[0], zh_ref[0])
    zc_s[CONV_HALO:CONV_HALO + tm, :] = z_ref[0]
    pc_s[0:POOL_HALO, :] = jnp.where(first, pi_ref[0], ph_ref[0])
    pc_s[POOL_HALO:POOL_HALO + tm, :] = p_ref[0]
    n_sh = zsh_s.shape[1]
    for k in range(1, SUBLANES):
        zsh_s[k] = zc_s[k:k + n_sh, :]

    width = zc_s.shape[1]
    rc = min(32, tm)
    for c in range(tm // rc):
        r0 = c * rc
        acc = jnp.broadcast_to(bdw_ref[...], (rc, width))
        for j in range(CONV_K):
            off = r0 + CONV_HALO - (CONV_K - 1) + j
            k, base = off % SUBLANES, off - off % SUBLANES
            tap = zc_s[base:base + rc, :] if k == 0 else zsh_s[k, base:base + rc, :]
            acc = acc + wdw_ref[j:j + 1, :] * tap
        mu = jnp.mean(acc, axis=-1, keepdims=True)
        dlt = acc - mu
        var = jnp.mean(dlt * dlt, axis=-1, keepdims=True)
        yn = dlt * lax.rsqrt(var + EPS) * lng_ref[...] + lnb_ref[...]
        yc_ref[0, r0:r0 + rc, :] = (yn * _sigmoid(yn)).astype(BF16)

    gw = width // len(POOL_WINDOWS)
    rp = min(128, tm)
    for c in range(tm // rp):
        r0 = c * rp
        pos = lax.broadcasted_iota(jnp.int32, (rp, 1), 0) + (pos0 + i * tm + r0)
        for g, w in enumerate(POOL_WINDOWS):
            lo = g * gw
            cur = pc_s[POOL_HALO + r0:POOL_HALO + r0 + rp, lo:lo + gw]
            acc = cur
            for s in range(1, w):
                acc = acc + pc_s[POOL_HALO + r0 - s:POOL_HALO + r0 - s + rp, lo:lo + gw]
            cnt = jnp.minimum(w, pos + 1).astype(F32)
            dlt = acc / cnt - cur
            m = _dot(dlt.astype(BF16), wg_ref[0, g]) * ps_ref[:, lo:lo + gw]
            mp_ref[0, r0:r0 + rp, lo:lo + gw] = m.astype(BF16)


def _convpool(z, z_halo_src, z_init, p, p_halo_src, p_init, w_dw, b_dw, ln_g, ln_b, w_grp, pool_scale,
              layer, tm, pos0):
    nb, tb, width = z.shape
    gw = width // len(POOL_WINDOWS)
    zsteps, psteps = tm // CONV_HALO, tm // POOL_HALO
    vec = pl.BlockSpec((1, width), lambda b, i: (0, 0))
    return pl.pallas_call(
        functools.partial(_convpool_kernel, tm=tm, pos0=pos0),
        out_shape=(jax.ShapeDtypeStruct((nb, tb, width), BF16), jax.ShapeDtypeStruct((nb, tb, width), BF16)),
        grid=(nb, tb // tm),
        in_specs=[
            pl.BlockSpec((1, tm, width), lambda b, i: (b, i, 0)),
            pl.BlockSpec((1, CONV_HALO, width), lambda b, i: (b, jnp.maximum(i * zsteps - 1, 0), 0)),
            pl.BlockSpec((1, CONV_HALO, width), lambda b, i: (b, 0, 0)),
            pl.BlockSpec((1, tm, width), lambda b, i: (b, i, 0)),
            pl.BlockSpec((1, POOL_HALO, width), lambda b, i: (b, jnp.maximum(i * psteps - 1, 0), 0)),
            pl.BlockSpec((1, POOL_HALO, width), lambda b, i: (b, 0, 0)),
            pl.BlockSpec((CONV_K, width), lambda b, i: (0, 0)),
            vec, vec, vec,
            pl.BlockSpec((1, len(POOL_WINDOWS), gw, gw), lambda b, i: (layer, 0, 0, 0)),
            vec,
        ],
        out_specs=(pl.BlockSpec((1, tm, width), lambda b, i: (b, i, 0)),
                   pl.BlockSpec((1, tm, width), lambda b, i: (b, i, 0))),
        scratch_shapes=[pltpu.VMEM((CONV_HALO + tm, width), F32),
                        pltpu.VMEM((SUBLANES, CONV_HALO + tm - SUBLANES, width), F32),
                        pltpu.VMEM((POOL_HALO + tm, width), F32)],
        compiler_params=_params(("parallel", "arbitrary")),
        name="convpool",
    )(z, z_halo_src, z_init, p, p_halo_src, p_init, w_dw, b_dw, ln_g, ln_b, w_grp, pool_scale)


def _attn_kernel(q_ref, k_ref, vt_ref, gs_ref, lam_ref, o_ref, qbd_s, s_s, m_s, acc_s, *, tq):
    kb = ATT_KB
    ks = ATT_KS
    gwid = 2 * kb
    ng = tq // kb
    assert tq == 2 * ks and ks == 2 * kb
    qi = pl.program_id(2)

    qt = q_ref[0].astype(F32).T
    row = lax.broadcasted_iota(jnp.int32, (V_DIM, kb), 0)
    for c in range(ng):
        blk = qt[:, c * kb:(c + 1) * kb]
        qbd_s[:, c * gwid:c * gwid + kb] = jnp.where(row < HEAD_DIM, blk, 0.0).astype(BF16)
        qbd_s[:, c * gwid + kb:(c + 1) * gwid] = jnp.where(row >= HEAD_DIM, blk, 0.0).astype(BF16)
    m_s[...] = jnp.full(m_s.shape, -jnp.inf, F32)
    acc_s[...] = jnp.zeros(acc_s.shape, F32)

    def scores(js, lo):
        keys = k_ref[0, pl.ds(pl.multiple_of(js * ks, ks), ks), :]
        return _dot(keys, qbd_s[:, lo:])

    def update(st, vt, lo):
        m_prev = m_s[:, lo:]
        m_new = jnp.maximum(m_prev, jnp.max(st, axis=0, keepdims=True))
        alpha = jnp.exp2(m_prev - m_new)
        pt = jnp.exp2(st - m_new).astype(BF16)
        acc_s[:, lo:] = acc_s[:, lo:] * alpha + _dot(vt, pt)
        m_s[:, lo:] = m_new

    n_full = 2 * qi
    s_s[0] = scores(0, 0)

    def pair(jj, carry):
        j0 = 2 * jj
        s_s[1] = scores(j0 + 1, 0)
        update(s_s[0], vt_ref[0, 0, j0], 0)
        s_s[0] = scores(j0 + 2, 0)
        update(s_s[1], vt_ref[0, 0, j0 + 1], 0)
        return carry

    def quad(jj, carry):
        pair(2 * jj, carry)
        return pair(2 * jj + 1, carry)

    lax.fori_loop(0, qi // 2, quad, 0)

    @pl.when(qi % 2 == 1)
    def _odd_pair():
        pair(qi - 1, 0)

    krow = lax.broadcasted_iota(jnp.int32, (kb, kb), 0)
    qcol = lax.broadcasted_iota(jnp.int32, (kb, kb), 1)
    shift = CHUNK.bit_length() - 1
    vis = lax.shift_right_logical(krow, shift) <= lax.shift_right_logical(qcol, shift)

    def masked(st):
        parts = [jnp.where(vis, st[:, :kb], -jnp.inf), jnp.where(vis, st[:, kb:gwid], -jnp.inf)]
        if st.shape[1] > gwid:
            parts.append(st[:, gwid:])
        return jnp.concatenate(parts, axis=1)

    s_s[1, :, 2 * gwid:] = scores(n_full + 1, 2 * gwid)
    for s in range(ng):
        buf, r0, lo = s // 2, (s % 2) * kb, s * gwid
        update(masked(s_s[buf, r0:r0 + kb, lo:]), vt_ref[0, 0, n_full + buf, :, r0:r0 + kb], lo)

    lam = lam_ref[0, 0]
    for c in range(ng):
        lo = c * gwid
        o1 = acc_s[0:V_DIM, lo:lo + kb] / acc_s[V_DIM:V_DIM + 1, lo:lo + kb]
        o2 = acc_s[0:V_DIM, lo + kb:lo + gwid] / acc_s[V_DIM:V_DIM + 1, lo + kb:lo + gwid]
        ot = o1 - lam * o2
        ms = jnp.mean(ot * ot, axis=0, keepdims=True)
        ot = ot * lax.rsqrt(ms + EPS) * gs_ref[...]
        o_ref[0, c * kb:(c + 1) * kb, :] = ot.T.astype(BF16)


def _attn(qb, kb, vt, g_sub_col, lam):
    nb, t, width = qb.shape
    tq = 2 * ATT_KS
    assert t % tq == 0
    return pl.pallas_call(
        functools.partial(_attn_kernel, tq=tq),
        out_shape=jax.ShapeDtypeStruct((nb, t, width), BF16),
        grid=(nb, N_HEADS, t // tq),
        in_specs=[
            pl.BlockSpec((1, tq, V_DIM), lambda b, h, i: (b, i, h)),
            pl.BlockSpec((1, t, V_DIM), lambda b, h, i: (b, 0, h)),
            pl.BlockSpec((1, 1, t // ATT_KS, V_ROWS, ATT_KS), lambda b, h, i: (b, h, 0, 0, 0)),
            pl.BlockSpec((V_DIM, 1), lambda b, h, i: (0, 0)),
            pl.BlockSpec(memory_space=pltpu.SMEM),
        ],
        out_specs=pl.BlockSpec((1, tq, V_DIM), lambda b, h, i: (b, i, h)),
        scratch_shapes=[
            pltpu.VMEM((V_DIM, 2 * tq), BF16),
            pltpu.VMEM((2, ATT_KS, 2 * tq), F32),
            pltpu.VMEM((1, 2 * tq), F32),
            pltpu.VMEM((V_ROWS, 2 * tq), F32),
        ],
        compiler_params=_params(("parallel", "parallel", "arbitrary")),
        name="attn",
    )(qb, kb, vt, g_sub_col, lam)


def _attn_s_kernel(q_ref, kn_ref, vn_ref, kc_ref, vc_ref, gs_ref, lam_ref, o_ref, qbd_s, m_s, l_s, acc_s,
                   *, ts, tk, n_kv):
    j = pl.program_id(1)

    def fold(h, s, v_bf16, m_prev, l_prev, acc_prev):
        m_new = jnp.maximum(m_prev, jnp.max(s, axis=1, keepdims=True))
        alpha = jnp.exp2(m_prev - m_new)
        p = jnp.exp2(s - m_new)
        l_s[h] = alpha * l_prev + jnp.sum(p, axis=1, keepdims=True)
        acc_s[h] = alpha * acc_prev + _dot(p.astype(BF16), v_bf16)
        m_s[h] = m_new

    @pl.when(j == 0)
    def _init():
        lane = lax.broadcasted_iota(jnp.int32, (ts, V_DIM), 1)
        pad_col = lax.broadcasted_iota(jnp.int32, (2 * ts, LANES), 1) >= ts
        for h in range(N_HEADS):
            cols = slice(h * V_DIM, (h + 1) * V_DIM)
            q = q_ref[0, :, cols].astype(F32)
            qbd_s[h, 0:ts, :] = jnp.where(lane < HEAD_DIM, q, 0.0).astype(BF16)
            qbd_s[h, ts:2 * ts, :] = jnp.where(lane >= HEAD_DIM, q, 0.0).astype(BF16)
            s = _dot_nt(qbd_s[h], kn_ref[0, :, cols])
            s = jnp.where(pad_col, -jnp.inf, s)
            fold(h, s, vn_ref[0, :, cols], jnp.full((2 * ts, 1), -jnp.inf, F32), jnp.zeros((2 * ts, 1), F32),
                 jnp.zeros((2 * ts, V_DIM), F32))

    for h in range(N_HEADS):
        s = _dot(qbd_s[h], kc_ref[0, 0, h].astype(BF16))
        v_h = vc_ref[0, 0, pl.ds(h, tk, stride=N_HEADS), :].astype(BF16)
        fold(h, s, v_h, m_s[h], l_s[h], acc_s[h])

    @pl.when(j == n_kv - 1)
    def _finish():
        lam = lam_ref[0, 0]
        for h in range(N_HEADS):
            o1 = acc_s[h, 0:ts, :] / l_s[h, 0:ts, :]
            o2 = acc_s[h, ts:2 * ts, :] / l_s[h, ts:2 * ts, :]
            o = o1 - lam * o2
            ms = jnp.mean(o * o, axis=-1, keepdims=True)
            o_ref[0, :, h * V_DIM:(h + 1) * V_DIM] = (o * lax.rsqrt(ms + EPS) * gs_ref[...]).astype(BF16)


def _attn_s(qb, kb_new, v_new, cache_k, cache_v, g_sub_row, lam, layer, tk):
    nb, ts, width = qb.shape
    past = cache_k.shape[4]
    n_kv = past // tk
    pad = ((0, 0), (0, LANES - ts), (0, 0))
    kb_new = jnp.pad(kb_new, pad)
    v_new = jnp.pad(v_new.astype(BF16), pad)
    new = pl.BlockSpec((1, ts, width), lambda b, j: (b, 0, 0))
    new_kv = pl.BlockSpec((1, LANES, width), lambda b, j: (b, 0, 0))
    old_k = pl.BlockSpec((1, 1, N_HEADS, V_DIM, tk), lambda b, j: (layer, b, 0, 0, j))
    old_v = pl.BlockSpec((1, 1, tk * N_HEADS, V_DIM), lambda b, j: (layer, b, j, 0))
    return pl.pallas_call(
        functools.partial(_attn_s_kernel, ts=ts, tk=tk, n_kv=n_kv),
        out_shape=jax.ShapeDtypeStruct((nb, ts, width), BF16),
        grid=(nb, n_kv),
        in_specs=[new, new_kv, new_kv, old_k, old_v,
                  pl.BlockSpec((1, V_DIM), lambda b, j: (0, 0)),
                  pl.BlockSpec(memory_space=pltpu.SMEM)],
        out_specs=new,
        scratch_shapes=[
            pltpu.VMEM((N_HEADS, 2 * ts, V_DIM), BF16),
            pltpu.VMEM((N_HEADS, 2 * ts, 1), F32),
            pltpu.VMEM((N_HEADS, 2 * ts, 1), F32),
            pltpu.VMEM((N_HEADS, 2 * ts, V_DIM), F32),
        ],
        compiler_params=_params(("parallel", "arbitrary")),
        name="attn_s",
    )(qb, kb_new, v_new, cache_k, cache_v, g_sub_row, lam)


def _merge_kernel(x_ref, yc_ref, mp_ref, o_ref, g_ref, sh_ref, sc_ref, g2_ref,
                  wg_ref, wc_ref, wp_ref, wa_ref, wo_ref, out_ref):
    x = x_ref[0]
    d = x.shape[1]
    h = _rms_mod(x, g_ref[...], sh_ref[0], sc_ref[0]).astype(BF16)
    merged = _sigmoid(_dot(h, wg_ref[0, :, 0:d])) * _dot(yc_ref[0], wc_ref[0])
    merged = merged + _sigmoid(_dot(h, wg_ref[0, :, d:2 * d])) * _dot(mp_ref[0], wp_ref[0])
    merged = merged + _sigmoid(_dot(h, wg_ref[0, :, 2 * d:3 * d])) * _dot(o_ref[0], wa_ref[0])
    out_ref[0] = x + g2_ref[0] * _dot(merged.astype(BF16), wo_ref[0])


def _merge(x, yc, mp, o, g, sh, sc, g2, w_gate, w_conv_out, w_pool_out, w_attn_out, w_out, layer, tm):
    nb, tb, d = x.shape
    rspec = lambda width: pl.BlockSpec((1, tm, width), lambda b, i: (b, i, 0))
    wspec = lambda w: _resident((1,) + w.shape[1:], lambda b, i: (layer, 0, 0))
    return pl.pallas_call(
        _merge_kernel,
        out_shape=jax.ShapeDtypeStruct(x.shape, F32),
        grid=(nb, tb // tm),
        in_specs=[
            rspec(d), rspec(yc.shape[2]), rspec(mp.shape[2]), rspec(o.shape[2]),
            pl.BlockSpec((1, d), lambda b, i: (0, 0)),
            _row_spec(sh, tm), _row_spec(sc, tm), _row_spec(g2, tm),
            wspec(w_gate), wspec(w_conv_out), wspec(w_pool_out), wspec(w_attn_out), wspec(w_out),
        ],
        out_specs=rspec(d),
        compiler_params=_params(("parallel", "parallel")),
        name="merge",
    )(x, yc, mp, o, g, sh, sc, g2, w_gate, w_conv_out, w_pool_out, w_attn_out, w_out)


def _rope_tables(pos):
    half = HEAD_DIM // 2
    inv = ROPE_THETA ** (-jnp.arange(half, dtype=F32) / half)
    ang = pos.astype(F32)[:, None] * inv[None, :]
    cos, sin = jnp.cos(ang), jnp.sin(ang)
    return jnp.tile(cos, (1, 4)), jnp.concatenate([-sin, sin, -sin, sin], axis=-1)


def _tile_rows(total, want):
    t = min(want, total)
    assert total % t == 0, (total, t)
    return t


def kernel(x_prompt, x_sample, cache_attn_k, cache_attn_v, state_conv, state_pool, c_prompt, c_sample, w_ada, b_ada, g_ffn1, w_ffn1_in, w_ffn1_out, g_mix, w_in, w_dw, b_dw, ln_conv_g, ln_conv_b, w_conv_out, w_pool_grp, pool_scale, w_pool_out, g_q, g_k, lam_q1, lam_k1, lam_q2, lam_k2, g_sub, w_attn_out, w_out, g_ffn2, w_ffn2_in, w_ffn2_out):
    bp, tp, d = x_prompt.shape
    bs, ts, _ = x_sample.shape
    n_layers = w_ada.shape[0]
    past = cache_attn_k.shape[2]
    conv_w = w_dw.shape[2]
    pool_w = pool_scale.shape[1]
    attn_w = N_HEADS * V_DIM
    proj_w = 2 * conv_w + pool_w + 3 * attn_w
    assert CONV_HALO >= CONV_K - 1 and POOL_HALO >= POOL_BUF and tp >= CONV_HALO

    w_ffn1_in_b, w_ffn1_out_b = w_ffn1_in.astype(BF16), w_ffn1_out.astype(BF16)
    w_ffn2_in_b, w_ffn2_out_b = w_ffn2_in.astype(BF16), w_ffn2_out.astype(BF16)
    w_proj_b = w_in[:, :, :proj_w].astype(BF16)
    w_gate_b = w_in[:, :, proj_w:].astype(BF16)
    w_conv_out_b, w_pool_out_b = w_conv_out.astype(BF16), w_pool_out.astype(BF16)
    w_attn_out_b, w_out_b = w_attn_out.astype(BF16), w_out.astype(BF16)
    w_grp_b = w_pool_grp.astype(BF16)
    chunk_id = np.arange(attn_w) // HEAD_DIM
    bd = jnp.asarray(chunk_id[:, None] == chunk_id[None, :], BF16)

    rows = bp + bs
    rows_pad = -(-rows // 16) * 16
    c_all = jnp.concatenate([c_prompt, c_sample, jnp.zeros((rows_pad - rows, d), F32)], axis=0)
    mod = _ada(c_all, w_ada, b_ada).reshape(n_layers, rows_pad, N_MOD, d)

    cos_p, sin_p = _rope_tables(jnp.arange(tp))
    cos_s, sin_s = _rope_tables(past + jnp.arange(ts))
    cos_p, sin_p = cos_p[None], sin_p[None]
    cos_s, sin_s = jnp.tile(cos_s, (bs, 1))[None], jnp.tile(sin_s, (bs, 1))[None]

    ms = bs * ts
    cache_k = jnp.transpose(cache_attn_k, (0, 1, 3, 4, 5, 2)).reshape(n_layers, bs, N_HEADS, V_DIM, past)
    cache_v = cache_attn_v.reshape(n_layers, bs, past * N_HEADS, V_DIM)
    tk_s = _tile_rows(past, 2048)

    tm_p = _tile_rows(tp, 512)
    tm_cp = _tile_rows(tp, 512)
    q_scale = HEAD_DIM ** -0.5 * math.log2(math.e)

    xp = x_prompt
    xs = x_sample.reshape(1, ms, d)
    kv_p = tuple(jnp.zeros((n_layers, bp, tp, attn_w), F32) for _ in range(2))
    kv_s = tuple(jnp.zeros((n_layers, 1, ms, attn_w), F32) for _ in range(2))
    outs = [[] for _ in range(4)]
    for l in range(n_layers):
        lam_init = 0.8 - 0.6 * math.exp(-0.3 * l)
        lam = (jnp.exp(jnp.sum(lam_q1[l] * lam_k1[l])) - jnp.exp(jnp.sum(lam_q2[l] * lam_k2[l])) + lam_init)
        lam = lam.reshape(1, 1).astype(F32)
        g_sub_eff = g_sub[l] * (1.0 - lam_init)
        gq_t = (jnp.tile(g_q[l], attn_w // HEAD_DIM) * q_scale)[None]
        gk_t = jnp.tile(g_k[l], attn_w // HEAD_DIM)[None]

        mod_p = [mod[l, :bp, k][:, None, :] for k in range(N_MOD)]
        mod_s = [jnp.repeat(mod[l, bp:rows, k], ts, axis=0)[None] for k in range(N_MOD)]

        def layer(x, m, tm, tm_c, cos, sin, conv_state, pool_state, pos0, kv_stacks, attend, want_vt):
            sh1, sc1, g1, sh2, sc2, g2, sh3, sc3, g3 = m
            tm_d = tm
            x = _ffn(x, g_ffn1[l][None], sh1, sc1, g1, w_ffn1_in_b, w_ffn1_out_b, l, tm_d)
            z, p, k_stack, v_stack, qb, kb, *vt = _inproj(
                x, g_mix[l][None], sh2, sc2, w_proj_b, bd, gq_t, gk_t, cos, sin, l, n_layers, tm,
                conv_w, pool_w, kv_stacks, want_vt)
            nb = conv_state.shape[0]
            zb = z.reshape(nb, -1, conv_w)
            pb = p.reshape(nb, -1, pool_w)
            z_init = jnp.pad(conv_state, ((0, 0), (CONV_HALO - (CONV_K - 1), 0), (0, 0)))
            p_init = jnp.pad(pool_state, ((0, 0), (POOL_HALO - POOL_BUF, 0), (0, 0)))
            halo_ok = zb.shape[1] >= CONV_HALO
            yc, mp = _convpool(zb, zb if halo_ok else z_init, z_init, pb, pb if halo_ok else p_init, p_init,
                               w_dw[l], b_dw[l][None], ln_conv_g[l][None], ln_conv_b[l][None],
                               w_grp_b, pool_scale[l][None], l, tm_c, pos0)
            o = attend(qb, kb, v_stack, *vt)
            x = _merge(x, yc.reshape(x.shape[0], -1, conv_w), mp.reshape(x.shape[0], -1, pool_w), o,
                       g_mix[l][None], sh2, sc2, g2, w_gate_b, w_conv_out_b, w_pool_out_b, w_attn_out_b,
                       w_out_b, l, tm_d)
            x = _ffn(x, g_ffn2[l][None], sh3, sc3, g3, w_ffn2_in_b, w_ffn2_out_b, l, tm_d)
            new_conv = jnp.concatenate([conv_state, zb], axis=1)[:, -(CONV_K - 1):]
            new_pool = jnp.concatenate([pool_state, pb], axis=1)[:, -POOL_BUF:]
            return x, (k_stack, v_stack), new_conv, new_pool

        def attend_p(qb, kb, v_stack, vt):
            return _attn(qb, kb, vt, g_sub_eff.reshape(V_DIM, 1), lam)

        def attend_s(qb, kb, v_stack):
            o = _attn_s(qb.reshape(bs, ts, attn_w), kb.reshape(bs, ts, attn_w), v_stack[l].reshape(bs, ts, attn_w),
                        cache_k, cache_v, g_sub_eff.reshape(1, V_DIM), lam, l, tk_s)
            return o.reshape(1, ms, attn_w)

        xp, kv_p, cp, pp = layer(xp, mod_p, tm_p, tm_cp, cos_p, sin_p,
                                 jnp.zeros((bp, CONV_K - 1, conv_w), F32),
                                 jnp.zeros((bp, POOL_BUF, pool_w), F32), 0, kv_p, attend_p, True)
        xs, kv_s, cn, pn = layer(xs, mod_s, ms, ts, cos_s, sin_s, state_conv[l], state_pool[l], past,
                                 kv_s, attend_s, False)
        for lst, val in zip(outs, (cp, pp, cn, pn)):
            lst.append(val)

    cp, pp, cn, pn = [jnp.stack(o) for o in outs]
    return (xp, xs.reshape(bs, ts, d),
            kv_p[0].reshape(n_layers, bp, tp, N_HEADS, 2, HEAD_DIM), kv_p[1].reshape(n_layers, bp, tp, N_HEADS, V_DIM),
            cp, pp,
            kv_s[0].reshape(n_layers, bs, ts, N_HEADS, 2, HEAD_DIM), kv_s[1].reshape(n_layers, bs, ts, N_HEADS, V_DIM),
            cn, pn)
```
